```python
import math
import jax, jax.numpy as jnp
from jax import lax
import numpy as np

D_MODEL = 1024
BATCH = 8
SEQ = 4096
DEPTH = 1

HEAD_DIM = 64
SWA_HEADS = 8
SWA_KV_HEADS = 2
WINDOW = 128
BLOCK = 128
DSA_HEADS = 8
DSA_KV_HEADS = 1
IDX_HEADS = 4
IDX_DIM = 64
TOPK_MAX = 256
D_MIX = (SWA_HEADS + DSA_HEADS) * HEAD_DIM
ROPE_THETA = 10000.0

MIX_SIZES = (
    SWA_HEADS * HEAD_DIM,
    SWA_KV_HEADS * HEAD_DIM,
    SWA_KV_HEADS * HEAD_DIM,
    DSA_HEADS * HEAD_DIM,
    DSA_KV_HEADS * HEAD_DIM,
    DSA_KV_HEADS * HEAD_DIM,
    IDX_HEADS * IDX_DIM,
    IDX_DIM,
    IDX_HEADS,
)
SPLIT_POINTS = [sum(MIX_SIZES[:i + 1]) for i in range(len(MIX_SIZES) - 1)]
VALUE_COLUMNS = (2, 5)

PEER_HEADS = 8
PEER_NKEYS = 128
PEER_DKEY = 256
PEER_TOPK = 16
PEER_EXPERTS = PEER_NKEYS * PEER_NKEYS
PEER_CHUNK = 128

ALPHA = (2.0 * DEPTH) ** 0.25
BETA = (8.0 * DEPTH) ** -0.25
LN_EPS = 1e-5

kernel_name = 'hymba_swa_sink_dsa_peer_deepnorm'


def _layer_norm(t, g, b):
    tf = t.astype(jnp.float32)
    mu = jnp.mean(tf, axis=-1, keepdims=True)
    var = jnp.mean(jnp.square(tf - mu), axis=-1, keepdims=True)
    y = (tf - mu) * lax.rsqrt(var + LN_EPS) * g.astype(jnp.float32) + b.astype(jnp.float32)
    return y.astype(t.dtype)


def _rope_tables(seq, dim):
    inv = ROPE_THETA ** (-jnp.arange(0, dim, 2, dtype=jnp.float32) / dim)
    ang = jnp.arange(seq, dtype=jnp.float32)[:, None] * inv[None, :]
    return jnp.cos(ang), jnp.sin(ang)


def _apply_rope(t, cos, sin):
    half = t.shape[-1] // 2
    t1, t2 = t[..., :half], t[..., half:]
    c = cos[:, None, :].astype(t.dtype)
    s = sin[:, None, :].astype(t.dtype)
    return jnp.concatenate([t1 * c - t2 * s, t2 * c + t1 * s], axis=-1)


def _swa_sink_attention(q, k, v, sinks):
    B, S, H, Dh = q.shape
    Hkv = k.shape[2]
    G = H // Hkv
    nb = S // BLOCK
    qb = q.reshape(B, nb, BLOCK, Hkv, G, Dh)

    def band_keys(t):
        tp = jnp.pad(t, ((0, 0), (BLOCK, 0), (0, 0), (0, 0)))
        prev = tp[:, :S].reshape(B, nb, BLOCK, Hkv, Dh)
        cur = t.reshape(B, nb, BLOCK, Hkv, Dh)
        return jnp.concatenate([prev, cur], axis=2)

    kb, vb = band_keys(k), band_keys(v)
    s = jnp.einsum('bnqkgd,bnskd->bnkgqs', qb, kb).astype(jnp.float32) * (Dh ** -0.5)
    qi = jnp.arange(BLOCK)[:, None]
    sj = jnp.arange(2 * BLOCK)[None, :]
    diff = qi + BLOCK - sj
    band = (diff >= 0) & (diff < WINDOW)
    blk = jnp.arange(nb)[:, None, None]
    valid = band[None] & ((blk > 0) | (sj[None] >= BLOCK))
    s = jnp.where(valid[None, :, None, None], s, -jnp.inf)
    sink = sinks.astype(jnp.float32).reshape(Hkv, G)
    sink_col = jnp.broadcast_to(sink[None, None, :, :, None, None], s.shape[:-1] + (1,))
    p = jax.nn.softmax(jnp.concatenate([s, sink_col], axis=-1), axis=-1)[..., :-1]
    o = jnp.einsum('bnkgqs,bnskd->bnqkgd', p.astype(v.dtype), vb)
    return o.reshape(B, S, H * Dh)


def _dsa_attention(q, k, v, q_idx, k_idx, w_idx):
    B, S, H, Dh = q.shape
    nb = S // BLOCK
    k_sel = min(TOPK_MAX, S // 4)
    spos = jnp.arange(S)
    w_scale = (IDX_HEADS ** -0.5) * (IDX_DIM ** -0.5)

    def to_blocks(a):
        return jnp.moveaxis(a.reshape((B, nb, BLOCK) + a.shape[2:]), 1, 0)

    def one_block(args):
        qb, qib, wib, t0 = args
        qpos = t0 + jnp.arange(BLOCK)
        rel = jax.nn.relu(jnp.einsum('bqhd,bsd->bqhs', qib, k_idx).astype(jnp.float32))
        iscore = jnp.einsum('bqh,bqhs->bqs', wib.astype(jnp.float32) * w_scale, rel)
        iscore = jnp.where(spos[None, None, :] <= qpos[None, :, None], iscore, -jnp.inf)
        _, sel = lax.top_k(iscore, k_sel)
        kg = jax.vmap(lambda kk, ii: kk[ii])(k, sel)
        vg = jax.vmap(lambda vv, ii: vv[ii])(v, sel)
        s = jnp.einsum('bqhd,bqkd->bhqk', qb, kg).astype(jnp.float32) * (Dh ** -0.5)
        valid = sel <= qpos[None, :, None]
        s = jnp.where(valid[:, None], s, -jnp.inf)
        p = jax.nn.softmax(s, axis=-1)
        return jnp.einsum('bhqk,bqkd->bqhd', p.astype(v.dtype), vg)

    starts = jnp.arange(nb) * BLOCK
    o = lax.map(one_block, (to_blocks(q), to_blocks(q_idx), to_blocks(w_idx), starts))
    return jnp.moveaxis(o, 0, 1).reshape(B, S, H * Dh)


def _peer(x, wq, subkeys, u_tab, v_tab):
    B, S, D = x.shape
    xc = x.reshape((B * S) // PEER_CHUNK, PEER_CHUNK, D)

    def chunk(xb):
        q = (xb @ wq).reshape(PEER_CHUNK, PEER_HEADS, 2, PEER_DKEY // 2)
        s = jnp.einsum('chpd,hpnd->chpn', q, subkeys).astype(jnp.float32)
        v1, i1 = lax.top_k(s[:, :, 0], PEER_TOPK)
        v2, i2 = lax.top_k(s[:, :, 1], PEER_TOPK)
        cand = (v1[..., :, None] + v2[..., None, :]).reshape(PEER_CHUNK, PEER_HEADS, PEER_TOPK * PEER_TOPK)
        sc, ci = lax.top_k(cand, PEER_TOPK)
        e = (jnp.take_along_axis(i1, ci // PEER_TOPK, axis=-1) * PEER_NKEYS
             + jnp.take_along_axis(i2, ci % PEER_TOPK, axis=-1))
        g = jax.nn.softmax(sc, axis=-1)
        u = u_tab[e]
        vv = v_tab[e]
        a = jax.nn.gelu(jnp.einsum('cd,chkd->chk', xb, u).astype(jnp.float32), approximate=False)
        return jnp.einsum('chk,chkd->cd', (g * a).astype(x.dtype), vv)

    return lax.map(chunk, xc).reshape(B, S, D)


def setup_inputs(seed: int = 0) -> dict:
    key = jax.random.key(seed)
    ks = jax.random.split(key, 32)
    f32 = jnp.float32
    x = jax.random.normal(ks[0], (BATCH, SEQ, D_MODEL), f32)
    pieces = []
    for i, n in enumerate(MIX_SIZES):
        w = jax.random.normal(ks[1 + i], (DEPTH, D_MODEL, n), f32) * (D_MODEL ** -0.5)
        if i in VALUE_COLUMNS:
            w = w * BETA
        pieces.append(w)
    w_in = jnp.concatenate(pieces, axis=-1)
    idx_k_g = 1.0 + 0.01 * jax.random.normal(ks[12], (DEPTH, IDX_DIM), f32)
    idx_k_b = 0.01 * jax.random.normal(ks[13], (DEPTH, IDX_DIM), f32)
    sinks = 0.5 * jax.random.normal(ks[14], (DEPTH, SWA_HEADS), f32)
    w_o = jax.random.normal(ks[15], (DEPTH, D_MIX, D_MODEL), f32) * (D_MIX ** -0.5) * BETA
    ln1_g = 1.0 + 0.01 * jax.random.normal(ks[16], (DEPTH, D_MODEL), f32)
    ln1_b = 0.01 * jax.random.normal(ks[17], (DEPTH, D_MODEL), f32)
    peer_wq = jax.random.normal(ks[18], (DEPTH, D_MODEL, PEER_HEADS * PEER_DKEY), f32) * (D_MODEL ** -0.5)
    peer_subkeys = jax.random.normal(ks[19], (DEPTH, PEER_HEADS, 2, PEER_NKEYS, PEER_DKEY // 2), f32) * ((PEER_DKEY // 2) ** -0.5)
    peer_u = jax.random.normal(ks[20], (DEPTH, PEER_EXPERTS, D_MODEL), f32) * (D_MODEL ** -0.5)
    peer_v = jax.random.normal(ks[21], (DEPTH, PEER_EXPERTS, D_MODEL), f32) * BETA
    ln2_g = 1.0 + 0.01 * jax.random.normal(ks[22], (DEPTH, D_MODEL), f32)
    ln2_b = 0.01 * jax.random.normal(ks[23], (DEPTH, D_MODEL), f32)
    return {'x': x, 'w_in': w_in, 'idx_k_g': idx_k_g, 'idx_k_b': idx_k_b, 'sinks': sinks,
            'w_o': w_o, 'ln1_g': ln1_g, 'ln1_b': ln1_b, 'peer_wq': peer_wq,
            'peer_subkeys': peer_subkeys, 'peer_u': peer_u, 'peer_v': peer_v,
            'ln2_g': ln2_g, 'ln2_b': ln2_b}


def reference(x, w_in, idx_k_g, idx_k_b, sinks, w_o, ln1_g, ln1_b, peer_wq,
              peer_subkeys, peer_u, peer_v, ln2_g, ln2_b):
    B, S, _ = x.shape
    cos, sin = _rope_tables(S, HEAD_DIM)
    h = x
    for l in range(DEPTH):
        proj = h @ w_in[l]
        qa, ka, va, qb, kb, vb, qi, ki, wi = jnp.split(proj, SPLIT_POINTS, axis=-1)
        qa = _apply_rope(qa.reshape(B, S, SWA_HEADS, HEAD_DIM), cos, sin)
        ka = _apply_rope(ka.reshape(B, S, SWA_KV_HEADS, HEAD_DIM), cos, sin)
        va = va.reshape(B, S, SWA_KV_HEADS, HEAD_DIM)
        qb = _apply_rope(qb.reshape(B, S, DSA_HEADS, HEAD_DIM), cos, sin)
        kb = _apply_rope(kb.reshape(B, S, 1, HEAD_DIM), cos, sin)[:, :, 0]
        qi = _apply_rope(qi.reshape(B, S, IDX_HEADS, IDX_DIM), cos, sin)
        ki = _layer_norm(ki, idx_k_g[l], idx_k_b[l])
        ki = _apply_rope(ki.reshape(B, S, 1, IDX_DIM), cos, sin)[:, :, 0]
        oa = _swa_sink_attention(qa, ka, va, sinks[l])
        ob = _dsa_attention(qb, kb, vb, qi, ki, wi)
        mix = jnp.concatenate([oa, ob], axis=-1) @ w_o[l]
        h = _layer_norm(ALPHA * h + mix, ln1_g[l], ln1_b[l])
        ffn = _peer(h, peer_wq[l], peer_subkeys[l], peer_u[l], peer_v[l])
        h = _layer_norm(ALPHA * h + ffn, ln2_g[l], ln2_b[l])
    return h
```

```python
import functools
import math

import jax
import jax.numpy as jnp
from jax import lax
from jax.experimental import pallas as pl
from jax.experimental.pallas import tpu as pltpu

F32 = jnp.float32
BF16 = jnp.bfloat16
I32 = jnp.int32

D_MODEL = 1024
HEAD_DIM = 64
SWA_HEADS = 8
SWA_KV_HEADS = 2
WINDOW = 128
BLOCK = 128
DSA_HEADS = 8
IDX_HEADS = 4
IDX_DIM = 64
TOPK_MAX = 256
ROPE_THETA = 10000.0
PEER_HEADS = 8
PEER_NKEYS = 128
PEER_DKEY = 256
PEER_TOPK = 16
LN_EPS = 1e-5
DEPTH = 1
ALPHA = (2.0 * DEPTH) ** 0.25

LANES = 128
VMEM_LIMIT_BYTES = 56 * 1024 * 1024

MIX_COLS = 1732
MIX_COLS_PAD = 1792
N_SLABS = MIX_COLS_PAD // LANES
WI_LANE0 = 64

NEG_BIG = -1e30
INT_MIN = -(2 ** 31)

_NT = (((1,), (1,)), ((), ()))


def _cparams(sem):
    return pltpu.CompilerParams(dimension_semantics=sem, vmem_limit_bytes=VMEM_LIMIT_BYTES)


def _layer_norm_rows(z, g, b):
    mu = jnp.mean(z, axis=-1, keepdims=True)
    zc = z - mu
    var = jnp.mean(zc * zc, axis=-1, keepdims=True)
    return zc * lax.rsqrt(var + LN_EPS) * g + b


def _in_proj_kernel(x_ref, w_ref, cos_ref, sin_ref, kg_ref, kb_ref,
                    qa_ref, ka_ref, va_ref, qb_ref, kb_out_ref, vb_ref, qi_ref, ki_ref, wi_ref):
    xb = x_ref[...].astype(BF16)
    proj = jnp.dot(xb, w_ref[...], preferred_element_type=F32)
    cos = cos_ref[...]
    sin = sin_ref[...]
    lane = lax.broadcasted_iota(I32, cos.shape, 1)
    first_half = (lane % HEAD_DIM) < (HEAD_DIM // 2)

    def rope(t):
        swapped = jnp.where(first_half, pltpu.roll(t, LANES - HEAD_DIM // 2, 1),
                            pltpu.roll(t, HEAD_DIM // 2, 1))
        return t * cos + swapped * sin

    def slab(i):
        return proj[:, i * LANES:(i + 1) * LANES]

    q_scale = HEAD_DIM ** -0.5
    for i in range(4):
        qa_ref[:, i * LANES:(i + 1) * LANES] = (rope(slab(i)) * q_scale).astype(BF16)
        qb_ref[:, i * LANES:(i + 1) * LANES] = (rope(slab(6 + i)) * q_scale).astype(BF16)
    ka_ref[...] = rope(slab(4)).astype(BF16)
    va_ref[...] = slab(5).astype(BF16)
    kv = slab(10)
    kv_r = rope(kv)
    kb_out_ref[...] = kv_r[:, :HEAD_DIM].astype(BF16)
    vb_ref[...] = kv[:, HEAD_DIM:].astype(BF16)
    for i in range(2):
        qi_ref[:, i * LANES:(i + 1) * LANES] = rope(slab(11 + i)).astype(BF16)
    last = slab(13)
    is_ki = lane < IDX_DIM
    mu = jnp.sum(jnp.where(is_ki, last, 0.0), axis=-1, keepdims=True) * (1.0 / IDX_DIM)
    cen = last - mu
    var = jnp.sum(jnp.where(is_ki, cen * cen, 0.0), axis=-1, keepdims=True) * (1.0 / IDX_DIM)
    kin = cen * lax.rsqrt(var + LN_EPS) * kg_ref[...] + kb_ref[...]
    kin = jnp.where(is_ki, kin, 0.0)
    ki_ref[...] = rope(kin)[:, :IDX_DIM].astype(BF16)
    w_scale = (IDX_HEADS ** -0.5) * (IDX_DIM ** -0.5)
    wi_ref[...] = last * w_scale


def _in_proj(x2, w_pad, cos_t, sin_t, kg, kb, seq, tm):
    n = x2.shape[0]
    nt = n // tm
    tps = seq // tm
    row = lambda w: pl.BlockSpec((tm, w), lambda i: (i, 0))
    out_shapes = [
        jax.ShapeDtypeStruct((n, 512), BF16), jax.ShapeDtypeStruct((n, 128), BF16),
        jax.ShapeDtypeStruct((n, 128), BF16), jax.ShapeDtypeStruct((n, 512), BF16),
        jax.ShapeDtypeStruct((n, 64), BF16), jax.ShapeDtypeStruct((n, 64), BF16),
        jax.ShapeDtypeStruct((n, 256), BF16), jax.ShapeDtypeStruct((n, 64), BF16),
        jax.ShapeDtypeStruct((n, 128), F32),
    ]
    return pl.pallas_call(
        _in_proj_kernel,
        grid=(nt,),
        in_specs=[
            row(D_MODEL),
            pl.BlockSpec((D_MODEL, MIX_COLS_PAD), lambda i: (0, 0)),
            pl.BlockSpec((tm, LANES), lambda i: (i % tps, 0)),
            pl.BlockSpec((tm, LANES), lambda i: (i % tps, 0)),
            pl.BlockSpec((1, LANES), lambda i: (0, 0)),
            pl.BlockSpec((1, LANES), lambda i: (0, 0)),
        ],
        out_specs=[row(512), row(128), row(128), row(512), row(64), row(64), row(256), row(64), row(128)],
        out_shape=out_shapes,
        compiler_params=_cparams(("parallel",)),
        name="in_proj",
    )(x2, w_pad, cos_t, sin_t, kg, kb)


def _swa_kernel(sink_ref, q_ref, kp_ref, kc_ref, vp_ref, vc_ref, o_ref):
    n = pl.program_id(1)
    qi = lax.broadcasted_iota(I32, (BLOCK, 2 * BLOCK), 0)
    sj = lax.broadcasted_iota(I32, (BLOCK, 2 * BLOCK), 1)
    diff = qi + BLOCK - sj
    valid = (diff >= 0) & (diff < WINDOW) & ((n > 0) | (sj >= BLOCK))
    group = SWA_HEADS // SWA_KV_HEADS
    for kh in range(SWA_KV_HEADS):
        cs = slice(kh * HEAD_DIM, (kh + 1) * HEAD_DIM)
        k = jnp.concatenate([kp_ref[:, cs], kc_ref[:, cs]], axis=0)
        v = jnp.concatenate([vp_ref[:, cs], vc_ref[:, cs]], axis=0)
        for g in range(group):
            h = kh * group + g
            hs = slice(h * HEAD_DIM, (h + 1) * HEAD_DIM)
            s = lax.dot_general(q_ref[:, hs], k, _NT, preferred_element_type=F32)
            s = jnp.where(valid, s, NEG_BIG)
            sink = sink_ref[h]
            m = jnp.maximum(jnp.max(s, axis=-1, keepdims=True), sink)
            p = jnp.exp(s - m)
            denom = jnp.sum(p, axis=-1, keepdims=True) + jnp.exp(sink - m)
            o = jnp.dot(p.astype(BF16), v, preferred_element_type=F32)
            o_ref[:, hs] = (o / denom).astype(o_ref.dtype)


def _swa(sinks, qa, ka, va, batch, seq):
    nb = seq // BLOCK
    cur = lambda w: pl.BlockSpec((BLOCK, w), lambda b, n: (b * nb + n, 0))
    prev = lambda w: pl.BlockSpec((BLOCK, w), lambda b, n: (b * nb + jnp.maximum(n - 1, 0), 0))
    return pl.pallas_call(
        _swa_kernel,
        grid=(batch, nb),
        in_specs=[pl.BlockSpec(memory_space=pltpu.SMEM), cur(512), prev(128), cur(128), prev(128), cur(128)],
        out_specs=cur(512),
        out_shape=jax.ShapeDtypeStruct((batch * seq, 512), BF16),
        compiler_params=_cparams(("parallel", "parallel")),
        name="swa",
    )(sinks, qa, ka, ka, va, va)


def _dsa_kernel(qb_ref, qi_ref, wi_ref, ki_ref, kb_ref, vb_ref, o_ref,
                keys_ref, tie_ref, m_ref, l_ref, acc_ref, *, ck, k_sel, seq_bits):
    n = pl.program_id(1)
    t0 = n * BLOCK
    nck = (t0 + BLOCK + ck - 1) // ck
    lane_tiles = ck // LANES
    qpos = t0 + lax.broadcasted_iota(I32, (BLOCK, ck), 0)
    lane_pos = lax.broadcasted_iota(I32, (BLOCK, ck), 1)

    qi = qi_ref[...]
    qi4 = jnp.concatenate([qi[:, h * IDX_DIM:(h + 1) * IDX_DIM] for h in range(IDX_HEADS)], axis=0)
    wi = wi_ref[...]
    wcols = [wi[:, WI_LANE0 + h:WI_LANE0 + h + 1] for h in range(IDX_HEADS)]

    def score_chunk(c, carry):
        r = lax.dot_general(qi4, ki_ref[c], _NT, preferred_element_type=F32)
        r = jnp.maximum(r, 0.0)
        isc = wcols[0] * r[0:BLOCK]
        for h in range(1, IDX_HEADS):
            isc = isc + wcols[h] * r[h * BLOCK:(h + 1) * BLOCK]
        isc = jnp.where(isc == 0.0, 0.0, isc)
        bits = pltpu.bitcast(isc, I32)
        key = bits ^ ((bits >> 31) & 0x7FFFFFFF)
        keys_ref[c] = jnp.where(c * ck + lane_pos <= qpos, key, INT_MIN)
        return carry

    lax.fori_loop(0, nck, score_chunk, 0)

    def count_rows(pred):
        def body(c, acc):
            key = keys_ref[c]
            hit = pred(key, c * ck + lane_pos)
            for j in range(lane_tiles):
                acc = acc + hit[:, j * LANES:(j + 1) * LANES].astype(I32)
            return acc
        acc = lax.fori_loop(0, nck, body, jnp.zeros((BLOCK, LANES), I32))
        return jnp.sum(acc, axis=1, keepdims=True)

    def bit_step(i, tu):
        cand_u = tu | lax.shift_left(jnp.int32(1), 31 - i)
        cand_s = cand_u ^ INT_MIN
        cnt = count_rows(lambda key, pos: key >= cand_s)
        return jnp.where(cnt >= k_sel, cand_u, tu)

    tu = lax.fori_loop(0, 32, bit_step, jnp.zeros((BLOCK, 1), I32))
    vstar = tu ^ INT_MIN

    cnt_gt = count_rows(lambda key, pos: key > vstar)
    cnt_eq = count_rows(lambda key, pos: key == vstar)
    need = k_sel - cnt_gt
    tie_ref[...] = jnp.full((BLOCK, 1), 2 ** seq_bits, I32)

    @pl.when(jnp.max(cnt_eq - need) > 0)
    def _():
        def idx_step(i, p):
            cand = p | lax.shift_left(jnp.int32(1), seq_bits - 1 - i)
            cnt = count_rows(lambda key, pos: (key == vstar) & (pos < cand))
            return jnp.where(cnt < need, cand, p)
        tie_ref[...] = lax.fori_loop(0, seq_bits, idx_step, jnp.zeros((BLOCK, 1), I32))

    tie_pos = tie_ref[...]

    qb = qb_ref[...]
    q8 = jnp.concatenate([qb[:, h * HEAD_DIM:(h + 1) * HEAD_DIM] for h in range(DSA_HEADS)], axis=0)
    m_ref[...] = jnp.full(m_ref.shape, NEG_BIG, F32)
    l_ref[...] = jnp.zeros(l_ref.shape, F32)
    acc_ref[...] = jnp.zeros(acc_ref.shape, F32)

    def attend_chunk(c, carry):
        s = lax.dot_general(q8, kb_ref[c], _NT, preferred_element_type=F32)
        key = keys_ref[c]
        pos = c * ck + lane_pos
        sel = ((key > vstar) | ((key == vstar) & (pos <= tie_pos))) & (key > INT_MIN)
        s3 = jnp.where(sel[None], s.reshape(DSA_HEADS, BLOCK, ck), NEG_BIG)
        m_old = m_ref[...]
        m_new = jnp.maximum(m_old, jnp.max(s3, axis=-1, keepdims=True))
        alpha = jnp.exp(m_old - m_new)
        p = jnp.exp(s3 - m_new)
        l_ref[...] = alpha * l_ref[...] + jnp.sum(p, axis=-1, keepdims=True)
        pv = jnp.dot(p.astype(BF16).reshape(DSA_HEADS * BLOCK, ck), vb_ref[c], preferred_element_type=F32)
        acc_ref[...] = alpha * acc_ref[...] + pv.reshape(DSA_HEADS, BLOCK, HEAD_DIM)
        m_ref[...] = m_new
        return carry

    lax.fori_loop(0, nck, attend_chunk, 0)
    o = acc_ref[...] / l_ref[...]
    for h in range(DSA_HEADS):
        o_ref[:, h * HEAD_DIM:(h + 1) * HEAD_DIM] = o[h].astype(o_ref.dtype)


def _dsa(qb, qi, wi, ki, kb, vb, batch, seq, ck):
    nb = seq // BLOCK
    nc = seq // ck
    k_sel = min(TOPK_MAX, seq // 4)
    seq_bits = int(math.log2(seq))
    assert 2 ** seq_bits == seq and k_sel <= ck
    chunked = lambda a: a.reshape(batch, nc, ck, a.shape[-1])
    blk = lambda w: pl.BlockSpec((BLOCK, w), lambda b, n: (b * nb + n, 0))
    whole = lambda w: pl.BlockSpec((None, nc, ck, w), lambda b, n: (b, 0, 0, 0))
    kernel = functools.partial(_dsa_kernel, ck=ck, k_sel=k_sel, seq_bits=seq_bits)
    return pl.pallas_call(
        kernel,
        grid=(batch, nb),
        in_specs=[blk(512), blk(256), blk(128), whole(IDX_DIM), whole(HEAD_DIM), whole(HEAD_DIM)],
        out_specs=blk(512),
        out_shape=jax.ShapeDtypeStruct((batch * seq, 512), BF16),
        scratch_shapes=[
            pltpu.VMEM((nc, BLOCK, ck), I32),
            pltpu.VMEM((BLOCK, 1), I32),
            pltpu.VMEM((DSA_HEADS, BLOCK, 1), F32),
            pltpu.VMEM((DSA_HEADS, BLOCK, 1), F32),
            pltpu.VMEM((DSA_HEADS, BLOCK, HEAD_DIM), F32),
        ],
        compiler_params=_cparams(("parallel", "arbitrary")),
        name="dsa",
    )(qb, qi, wi, chunked(ki), chunked(kb), chunked(vb))


def _out_proj_kernel(oa_ref, ob_ref, x_ref, wa_ref, wb_ref, g_ref, b_ref, h_ref, hb_ref):
    mix = jnp.dot(oa_ref[...], wa_ref[...], preferred_element_type=F32)
    mix = mix + jnp.dot(ob_ref[...], wb_ref[...], preferred_element_type=F32)
    h = _layer_norm_rows(ALPHA * x_ref[...] + mix, g_ref[...], b_ref[...])
    h_ref[...] = h
    hb_ref[...] = h.astype(BF16)


def _out_proj(oa, ob, x2, wo_a, wo_b, g, b, tm):
    n = x2.shape[0]
    row = lambda w: pl.BlockSpec((tm, w), lambda i: (i, 0))
    const = lambda r, c: pl.BlockSpec((r, c), lambda i: (0, 0))
    return pl.pallas_call(
        _out_proj_kernel,
        grid=(n // tm,),
        in_specs=[row(512), row(512), row(D_MODEL), const(512, D_MODEL), const(512, D_MODEL),
                  const(1, D_MODEL), const(1, D_MODEL)],
        out_specs=[row(D_MODEL), row(D_MODEL)],
        out_shape=[jax.ShapeDtypeStruct((n, D_MODEL), F32), jax.ShapeDtypeStruct((n, D_MODEL), BF16)],
        compiler_params=_cparams(("parallel",)),
        name="out_proj",
    )(oa, ob, x2, wo_a, wo_b, g, b)


def _top_values(s, count):
    vals = []
    rank = jnp.zeros(s.shape, F32)
    cur = s
    for _ in range(count):
        m = jnp.max(cur, axis=0, keepdims=True)
        vals.append(m)
        cur = jnp.where(cur < m, cur, -jnp.inf)
        rank = rank + (s < m).astype(F32)
    return vals, rank


def _peer_route_kernel(h_ref, wqt_ref, sk_ref, r2_ref, p2_ref, na_ref, p1_ref):
    qt = lax.dot_general(wqt_ref[...], h_ref[...], _NT, preferred_element_type=F32)
    half = PEER_DKEY // 2
    for h in range(PEER_HEADS):
        s = []
        for p in range(2):
            idx = h * 2 + p
            qhp = qt[idx * half:(idx + 1) * half, :].astype(BF16)
            s.append(jnp.dot(sk_ref[idx], qhp, preferred_element_type=F32))
        s1, s2 = s
        v1, _ = _top_values(s1, PEER_TOPK)
        v2, rank2 = _top_values(s2, PEER_TOPK)
        v2_all = jnp.concatenate(v2, axis=0)
        sub = lax.broadcasted_iota(I32, (8, v2_all.shape[1]), 0)
        cands = [v1[0] + v2_all]
        for i in range(1, 8):
            lim = PEER_TOPK // (i + 1)
            cands.append(jnp.where(sub < lim, v1[i] + v2_all[:8], -jnp.inf))
        cands.append(jnp.concatenate(v1[8:], axis=0) + v2[0])
        cand = jnp.concatenate(cands, axis=0)
        tops, _ = _top_values(cand, PEER_TOPK)
        tau = tops[-1]
        tau = jnp.where(tau > -jnp.inf, tau, jnp.min(jnp.where(cand > -jnp.inf, cand, jnp.inf), axis=0,
                                                     keepdims=True))
        mx = v1[0] + v2[0]
        z = jnp.sum(jnp.where(cand >= tau, jnp.exp(cand - mx), 0.0), axis=0, keepdims=True)
        na = jnp.zeros(s1.shape, F32)
        for j in range(PEER_TOPK):
            na = na + ((s1 + v2[j]) >= tau).astype(F32)
        r2_ref[h] = rank2
        p2_ref[h] = jnp.exp(s2 - v2[0])
        na_ref[h] = na
        p1_ref[h] = jnp.exp(s1 - v1[0]) / z


def _peer_route(hb, wqt, sk, tb):
    n = hb.shape[0]
    tab = jax.ShapeDtypeStruct((PEER_HEADS, PEER_NKEYS, n), F32)
    tab_spec = pl.BlockSpec((PEER_HEADS, PEER_NKEYS, tb), lambda i: (0, 0, i))
    return pl.pallas_call(
        _peer_route_kernel,
        grid=(n // tb,),
        in_specs=[pl.BlockSpec((tb, D_MODEL), lambda i: (i, 0)),
                  pl.BlockSpec(wqt.shape, lambda i: (0, 0)),
                  pl.BlockSpec(sk.shape, lambda i: (0, 0, 0))],
        out_specs=[tab_spec] * 4,
        out_shape=[tab] * 4,
        compiler_params=_cparams(("parallel",)),
        name="peer_route",
    )(hb, wqt, sk)


def _peer_ffn_kernel(hb_ref, h_ref, u_ref, vt_ref, r2_ref, p2_ref, na_ref, p1_ref, g_ref, b_ref,
                     o_ref, acc_ref, w_ref, *, tb, eb):
    j = pl.program_id(1)

    @pl.when(j == 0)
    def _():
        acc_ref[...] = jnp.zeros(acc_ref.shape, F32)

    act_t = lax.dot_general(u_ref[...], hb_ref[...], _NT, preferred_element_type=F32)
    a_per_step = eb // PEER_NKEYS
    for al in range(a_per_step):
        rows = slice(al * PEER_NKEYS, (al + 1) * PEER_NKEYS)
        for tt in range(tb // LANES):
            cols = slice(tt * LANES, (tt + 1) * LANES)
            gate = jnp.zeros((PEER_NKEYS, LANES), F32)
            for h in range(PEER_HEADS):
                na = na_ref[h, al:al + 1, cols]
                p1 = p1_ref[h, al:al + 1, cols]
                gate = gate + jnp.where(r2_ref[h, :, cols] < na, p2_ref[h, :, cols], 0.0) * p1
            x = act_t[rows, cols]
            gelu = 0.5 * x * (1.0 + lax.erf(x * (2.0 ** -0.5)))
            w_ref[rows, cols] = (gate * gelu).astype(BF16)
    acc_ref[...] += jnp.dot(vt_ref[...], w_ref[...], preferred_element_type=F32)

    @pl.when(j == pl.num_programs(1) - 1)
    def _():
        z = ALPHA * h_ref[...] + acc_ref[...].T
        o_ref[...] = _layer_norm_rows(z, g_ref[...], b_ref[...])


def _peer_ffn(hb, h, u, vt, tabs, g, b, tb, eb):
    n = hb.shape[0]
    ne = u.shape[0]
    tab_spec = pl.BlockSpec((PEER_HEADS, PEER_NKEYS, tb), lambda i, j: (0, 0, i))
    a_spec = pl.BlockSpec((PEER_HEADS, eb // PEER_NKEYS, tb), lambda i, j: (0, j, i))
    kernel = functools.partial(_peer_ffn_kernel, tb=tb, eb=eb)
    return pl.pallas_call(
        kernel,
        grid=(n // tb, ne // eb),
        in_specs=[pl.BlockSpec((tb, D_MODEL), lambda i, j: (i, 0)),
                  pl.BlockSpec((tb, D_MODEL), lambda i, j: (i, 0)),
                  pl.BlockSpec((eb, D_MODEL), lambda i, j: (j, 0)),
                  pl.BlockSpec((D_MODEL, eb), lambda i, j: (0, j)),
                  tab_spec, tab_spec, a_spec, a_spec,
                  pl.BlockSpec((1, D_MODEL), lambda i, j: (0, 0)),
                  pl.BlockSpec((1, D_MODEL), lambda i, j: (0, 0))],
        out_specs=pl.BlockSpec((tb, D_MODEL), lambda i, j: (i, 0)),
        out_shape=jax.ShapeDtypeStruct((n, D_MODEL), F32),
        scratch_shapes=[pltpu.VMEM((D_MODEL, tb), F32), pltpu.VMEM((eb, tb), BF16)],
        compiler_params=_cparams(("parallel", "arbitrary")),
        name="peer_ffn",
    )(hb, h, u, vt, *tabs, g, b)


def _tiles(n_tokens, seq):
    tm = 256 if seq % 256 == 0 else BLOCK
    ck = 512 if seq >= 2048 else BLOCK
    tb_route = 256 if n_tokens % 256 == 0 else BLOCK
    tb_ffn = 512 if n_tokens % 512 == 0 else BLOCK
    eb = 1024
    return tm, ck, tb_route, tb_ffn, eb


def kernel(x, w_in, idx_k_g, idx_k_b, sinks, w_o, ln1_g, ln1_b, peer_wq, peer_subkeys, peer_u, peer_v,
           ln2_g, ln2_b):
    batch, seq, _ = x.shape
    n = batch * seq
    tm, ck, tb_route, tb_ffn, eb = _tiles(n, seq)

    inv = ROPE_THETA ** (-jnp.arange(0, HEAD_DIM, 2, dtype=F32) / HEAD_DIM)
    ang = jnp.arange(seq, dtype=F32)[:, None] * inv[None, :]
    cos, sin = jnp.cos(ang), jnp.sin(ang)
    cos_t = jnp.concatenate([cos, cos, cos, cos], axis=-1)
    sin_t = jnp.concatenate([-sin, sin, -sin, sin], axis=-1)

    h = x.reshape(n, D_MODEL)
    for l in range(DEPTH):
        w_pad = jnp.pad(w_in[l], ((0, 0), (0, MIX_COLS_PAD - MIX_COLS))).astype(BF16)
        kg = jnp.pad(idx_k_g[l], (0, LANES - IDX_DIM)).reshape(1, LANES)
        kb = jnp.pad(idx_k_b[l], (0, LANES - IDX_DIM)).reshape(1, LANES)
        qa, ka, va, qb, kbb, vb, qi, ki, wi = _in_proj(h, w_pad, cos_t, sin_t, kg, kb, seq, tm)
        oa = _swa(sinks[l], qa, ka, va, batch, seq)
        ob = _dsa(qb, qi, wi, ki, kbb, vb, batch, seq, ck)
        wo = w_o[l].astype(BF16)
        h1, h1b = _out_proj(oa, ob, h, wo[:512], wo[512:], ln1_g[l].reshape(1, -1), ln1_b[l].reshape(1, -1), tm)
        wqt = peer_wq[l].T.astype(BF16)
        sk = peer_subkeys[l].reshape(PEER_HEADS * 2, PEER_NKEYS, PEER_DKEY // 2).astype(BF16)
        tabs = _peer_route(h1b, wqt, sk, tb_route)
        u = peer_u[l].astype(BF16)
        vt = peer_v[l].T.astype(BF16)
        h = _peer_ffn(h1b, h1, u, vt, tabs, ln2_g[l].reshape(1, -1), ln2_b[l].reshape(1, -1), tb_ffn, eb)
    return h.reshape(batch, seq, D_MODEL)
```

```python
import functools
import math

import jax
import jax.numpy as jnp
from jax import lax
from jax.experimental import pallas as pl
from jax.experimental.pallas import tpu as pltpu

F32 = jnp.float32
BF16 = jnp.bfloat16
I32 = jnp.int32

D_MODEL = 1024
HEAD_DIM = 64
SWA_HEADS = 8
SWA_KV_HEADS = 2
WINDOW = 128
BLOCK = 128
DSA_HEADS = 8
IDX_HEADS = 4
IDX_DIM = 64
TOPK_MAX = 256
ROPE_THETA = 10000.0
PEER_HEADS = 8
PEER_NKEYS = 128
PEER_DKEY = 256
PEER_TOPK = 16
LN_EPS = 1e-5
DEPTH = 1
ALPHA = (2.0 * DEPTH) ** 0.25

LANES = 128
BF16_ROWS = 16
VMEM_LIMIT_BYTES = 56 * 1024 * 1024

MIX_COLS = 1732
MIX_COLS_PAD = 1792
WI_LANE0 = 64

NEG_BIG = -1e30
INT_MIN = -(2 ** 31)

_NT = (((1,), (1,)), ((), ()))


def _cparams(sem):
    return pltpu.CompilerParams(dimension_semantics=sem, vmem_limit_bytes=VMEM_LIMIT_BYTES)


def _layer_norm_rows(z, g, b):
    mu = jnp.mean(z, axis=-1, keepdims=True)
    zc = z - mu
    var = jnp.mean(zc * zc, axis=-1, keepdims=True)
    return zc * lax.rsqrt(var + LN_EPS) * g + b


def _in_proj_kernel(x_ref, w_ref, cos_ref, sin_ref, kg_ref, kb_ref,
                    qa_ref, ka_ref, va_ref, qb_ref, kb_out_ref, vb_ref, qi_ref, ki_ref, wi_ref):
    xb = x_ref[...].astype(BF16)
    proj = jnp.dot(xb, w_ref[...], preferred_element_type=F32)
    cos = cos_ref[...]
    sin = sin_ref[...]
    lane = lax.broadcasted_iota(I32, cos.shape, 1)
    first_half = (lane % HEAD_DIM) < (HEAD_DIM // 2)

    def rope(t):
        swapped = jnp.where(first_half, pltpu.roll(t, LANES - HEAD_DIM // 2, 1),
                            pltpu.roll(t, HEAD_DIM // 2, 1))
        return t * cos + swapped * sin

    def slab(i):
        return proj[:, i * LANES:(i + 1) * LANES]

    q_scale = HEAD_DIM ** -0.5
    for i in range(4):
        qa_ref[:, i * LANES:(i + 1) * LANES] = (rope(slab(i)) * q_scale).astype(BF16)
        qb_ref[:, i * LANES:(i + 1) * LANES] = (rope(slab(6 + i)) * q_scale).astype(BF16)
    ka_ref[...] = rope(slab(4)).astype(BF16)
    va_ref[...] = slab(5).astype(BF16)
    kv = slab(10)
    kv_r = rope(kv)
    kb_out_ref[...] = kv_r[:, :HEAD_DIM].astype(BF16)
    vb_ref[...] = jnp.where(lane < HEAD_DIM, pltpu.roll(kv, HEAD_DIM, 1),
                            jnp.where(lane == HEAD_DIM, 1.0, 0.0)).astype(BF16)
    for i in range(2):
        qi_ref[:, i * LANES:(i + 1) * LANES] = rope(slab(11 + i)).astype(BF16)
    last = slab(13)
    is_ki = lane < IDX_DIM
    mu = jnp.sum(jnp.where(is_ki, last, 0.0), axis=-1, keepdims=True) * (1.0 / IDX_DIM)
    cen = last - mu
    var = jnp.sum(jnp.where(is_ki, cen * cen, 0.0), axis=-1, keepdims=True) * (1.0 / IDX_DIM)
    kin = cen * lax.rsqrt(var + LN_EPS) * kg_ref[...] + kb_ref[...]
    kin = jnp.where(is_ki, kin, 0.0)
    ki_ref[...] = rope(kin)[:, :IDX_DIM].astype(BF16)
    w_scale = (IDX_HEADS ** -0.5) * (IDX_DIM ** -0.5)
    wi_ref[...] = last * w_scale


def _in_proj(x2, w_pad, cos_t, sin_t, kg, kb, seq, tm):
    n = x2.shape[0]
    nt = n // tm
    tps = seq // tm
    row = lambda w: pl.BlockSpec((tm, w), lambda i: (i, 0))
    out_shapes = [
        jax.ShapeDtypeStruct((n, 512), BF16), jax.ShapeDtypeStruct((n, 128), BF16),
        jax.ShapeDtypeStruct((n, 128), BF16), jax.ShapeDtypeStruct((n, 512), BF16),
        jax.ShapeDtypeStruct((n, 64), BF16), jax.ShapeDtypeStruct((n, 128), BF16),
        jax.ShapeDtypeStruct((n, 256), BF16), jax.ShapeDtypeStruct((n, 64), BF16),
        jax.ShapeDtypeStruct((n, 128), F32),
    ]
    return pl.pallas_call(
        _in_proj_kernel,
        grid=(nt,),
        in_specs=[
            row(D_MODEL),
            pl.BlockSpec((D_MODEL, MIX_COLS_PAD), lambda i: (0, 0)),
            pl.BlockSpec((tm, LANES), lambda i: (i % tps, 0)),
            pl.BlockSpec((tm, LANES), lambda i: (i % tps, 0)),
            pl.BlockSpec((1, LANES), lambda i: (0, 0)),
            pl.BlockSpec((1, LANES), lambda i: (0, 0)),
        ],
        out_specs=[row(512), row(128), row(128), row(512), row(64), row(128), row(256), row(64), row(128)],
        out_shape=out_shapes,
        compiler_params=_cparams(("parallel",)),
        name="in_proj",
    )(x2, w_pad, cos_t, sin_t, kg, kb)


def _swa_kernel(sink_ref, q_ref, kp_ref, kc_ref, vp_ref, vc_ref, o_ref):
    n = pl.program_id(1)
    qi = lax.broadcasted_iota(I32, (BLOCK, 2 * BLOCK), 0)
    sj = lax.broadcasted_iota(I32, (BLOCK, 2 * BLOCK), 1)
    diff = qi + BLOCK - sj
    valid = (diff >= 0) & (diff < WINDOW) & ((n > 0) | (sj >= BLOCK))
    group = SWA_HEADS // SWA_KV_HEADS
    for kh in range(SWA_KV_HEADS):
        cs = slice(kh * HEAD_DIM, (kh + 1) * HEAD_DIM)
        k = jnp.concatenate([kp_ref[:, cs], kc_ref[:, cs]], axis=0)
        v = jnp.concatenate([vp_ref[:, cs], vc_ref[:, cs]], axis=0)
        for g in range(group):
            h = kh * group + g
            hs = slice(h * HEAD_DIM, (h + 1) * HEAD_DIM)
            s = lax.dot_general(q_ref[:, hs], k, _NT, preferred_element_type=F32)
            s = jnp.where(valid, s, NEG_BIG)
            sink = sink_ref[h]
            m = jnp.maximum(jnp.max(s, axis=-1, keepdims=True), sink)
            p = jnp.exp(s - m)
            denom = jnp.sum(p, axis=-1, keepdims=True) + jnp.exp(sink - m)
            o = jnp.dot(p.astype(BF16), v, preferred_element_type=F32)
            o_ref[:, hs] = (o / denom).astype(o_ref.dtype)


def _swa(sinks, qa, ka, va, batch, seq):
    nb = seq // BLOCK
    cur = lambda w: pl.BlockSpec((BLOCK, w), lambda b, n: (b * nb + n, 0))
    prev = lambda w: pl.BlockSpec((BLOCK, w), lambda b, n: (b * nb + jnp.maximum(n - 1, 0), 0))
    return pl.pallas_call(
        _swa_kernel,
        grid=(batch, nb),
        in_specs=[pl.BlockSpec(memory_space=pltpu.SMEM), cur(512), prev(128), cur(128), prev(128), cur(128)],
        out_specs=cur(512),
        out_shape=jax.ShapeDtypeStruct((batch * seq, 512), BF16),
        compiler_params=_cparams(("parallel", "parallel")),
        name="swa",
    )(sinks, qa, ka, ka, va, va)


DSA_ROWS = 2 * BLOCK


def _dsa_kernel(qb_ref, qi_ref, wi_ref, ki_ref, kb_ref, vb_ref, o_ref,
                keys_ref, vst_ref, tie_ref, bias_ref, m_ref, acc_ref, *, ck, nc, k_sel, seq_bits):
    n = pl.program_id(1)
    t0 = n * DSA_ROWS
    nck = (t0 + DSA_ROWS + ck - 1) // ck
    lane_tiles = ck // LANES
    qpos = t0 + lax.broadcasted_iota(I32, (DSA_ROWS, ck), 0)
    lane_pos = lax.broadcasted_iota(I32, (DSA_ROWS, ck), 1)
    tile_lane = lax.broadcasted_iota(I32, (BLOCK, LANES), 1)
    ones = jnp.ones((LANES, LANES), BF16)

    qi = qi_ref[...]
    qi4 = jnp.concatenate([qi[:, h * IDX_DIM:(h + 1) * IDX_DIM] for h in range(IDX_HEADS)], axis=0)
    wi = wi_ref[...]
    wcols = [wi[:, WI_LANE0 + h:WI_LANE0 + h + 1] for h in range(IDX_HEADS)]

    def score_chunk(c, carry):
        r = lax.dot_general(qi4, ki_ref[c], _NT, preferred_element_type=F32)
        r = jnp.maximum(r, 0.0)
        isc = wcols[0] * r[0:DSA_ROWS]
        for h in range(1, IDX_HEADS):
            isc = isc + wcols[h] * r[h * DSA_ROWS:(h + 1) * DSA_ROWS]
        isc = jnp.where(isc == 0.0, 0.0, isc)
        bits = pltpu.bitcast(isc, I32)
        key = bits ^ ((bits >> 31) & 0x7FFFFFFF)
        keys_ref[c] = jnp.where(c * ck + lane_pos <= qpos, key, INT_MIN)
        return carry

    lax.fori_loop(0, nck, score_chunk, 0)

    def row_total(acc):
        return jnp.dot(acc.astype(BF16), ones, preferred_element_type=F32)

    def select(v):
        def count(grp, pred):
            acc = jnp.zeros((BLOCK, LANES), F32)
            for c in range(v):
                for j in range(lane_tiles):
                    key = keys_ref[c, grp * BLOCK:(grp + 1) * BLOCK, j * LANES:(j + 1) * LANES]
                    acc = acc + jnp.where(pred(key, c * ck + j * LANES + tile_lane), 1.0, 0.0)
            return acc

        def bit_step(i, carry):
            ta, tb, cand_b_prev, acc_b = carry
            bit = lax.shift_left(jnp.int32(1), 31 - i)
            tot_b = row_total(acc_b)
            cand_a = ta | bit
            cand_a_s = cand_a ^ INT_MIN
            acc_a = count(0, lambda key, pos: key >= cand_a_s)
            tb = jnp.where(tot_b >= k_sel, cand_b_prev, tb)
            cand_b = tb | bit
            cand_b_s = cand_b ^ INT_MIN
            acc_b = count(1, lambda key, pos: key >= cand_b_s)
            ta = jnp.where(row_total(acc_a) >= k_sel, cand_a, ta)
            return ta, tb, cand_b, acc_b

        zi = jnp.zeros((BLOCK, LANES), I32)
        ta, tb, cand_b, acc_b = lax.fori_loop(0, 32, bit_step, (zi, zi, zi, jnp.zeros((BLOCK, LANES), F32)))
        tb = jnp.where(row_total(acc_b) >= k_sel, cand_b, tb)
        vstar = (ta ^ INT_MIN, tb ^ INT_MIN)
        big = jnp.full((BLOCK, LANES), 2 ** seq_bits, I32)
        need, surplus = [], []
        for grp in range(2):
            vs = vstar[grp]
            vst_ref[grp * BLOCK:(grp + 1) * BLOCK, :] = vs
            tie_ref[grp * BLOCK:(grp + 1) * BLOCK, :] = big
            cnt_gt = row_total(count(grp, lambda key, pos: key > vs))
            cnt_eq = row_total(count(grp, lambda key, pos: key == vs))
            need.append(k_sel - cnt_gt)
            surplus.append(jnp.max(cnt_eq - need[grp]))

        @pl.when(jnp.maximum(surplus[0], surplus[1]) > 0.0)
        def _():
            for grp in range(2):
                vs = vstar[grp]

                def idx_step(i, p):
                    cand = p | lax.shift_left(jnp.int32(1), seq_bits - 1 - i)
                    cnt = row_total(count(grp, lambda key, pos: (key == vs) & (pos < cand)))
                    return jnp.where(cnt < need[grp], cand, p)

                tie_ref[grp * BLOCK:(grp + 1) * BLOCK, :] = lax.fori_loop(0, seq_bits, idx_step, zi)

    for v in range(1, nc + 1):
        pl.when(nck == v)(functools.partial(select, v))

    qb = qb_ref[...]
    q_heads = [qb[:, h * HEAD_DIM:(h + 1) * HEAD_DIM] for h in range(DSA_HEADS)]
    m_ref[...] = jnp.full(m_ref.shape, NEG_BIG, F32)
    acc_ref[...] = jnp.zeros(acc_ref.shape, F32)
    row_lane = lax.broadcasted_iota(I32, (DSA_ROWS, LANES), 1)

    def attend_chunk(c, carry):
        vst = vst_ref[...]
        tie = tie_ref[...]
        for j in range(lane_tiles):
            key = keys_ref[c, :, j * LANES:(j + 1) * LANES]
            pos = c * ck + j * LANES + row_lane
            sel = ((key > vst) | ((key == vst) & (pos <= tie))) & (key > INT_MIN)
            bias_ref[:, j * LANES:(j + 1) * LANES] = jnp.where(sel, 0.0, NEG_BIG)
        kc = kb_ref[c]
        vc = vb_ref[c]
        for h in range(DSA_HEADS):
            s = lax.dot_general(q_heads[h], kc, _NT, preferred_element_type=F32) + bias_ref[...]
            m_old = m_ref[h]
            m_new = jnp.maximum(m_old, jnp.max(s, axis=-1, keepdims=True))
            p = jnp.exp(s - m_new).astype(BF16)
            acc_ref[h] = jnp.exp(m_old - m_new) * acc_ref[h] + jnp.dot(p, vc, preferred_element_type=F32)
            m_ref[h] = m_new
        return carry

    lax.fori_loop(0, nck, attend_chunk, 0)
    for h in range(DSA_HEADS):
        a = acc_ref[h]
        o_ref[:, h * HEAD_DIM:(h + 1) * HEAD_DIM] = (a[:, :HEAD_DIM] / a[:, HEAD_DIM:HEAD_DIM + 1]).astype(o_ref.dtype)


def _dsa(qb, qi, wi, ki, kb, vb, batch, seq, ck):
    nq = seq // DSA_ROWS
    nc = seq // ck
    k_sel = min(TOPK_MAX, seq // 4)
    seq_bits = int(math.log2(seq))
    assert 2 ** seq_bits == seq and k_sel <= ck and seq % DSA_ROWS == 0
    assert seq // LANES <= 256
    chunked = lambda a: a.reshape(batch, nc, ck, a.shape[-1])
    blk = lambda w: pl.BlockSpec((DSA_ROWS, w), lambda b, n: (b * nq + n, 0))
    whole = lambda w: pl.BlockSpec((None, nc, ck, w), lambda b, n: (b, 0, 0, 0))
    kernel = functools.partial(_dsa_kernel, ck=ck, nc=nc, k_sel=k_sel, seq_bits=seq_bits)
    return pl.pallas_call(
        kernel,
        grid=(batch, nq),
        in_specs=[blk(512), blk(256), blk(128), whole(IDX_DIM), whole(HEAD_DIM), whole(LANES)],
        out_specs=blk(512),
        out_shape=jax.ShapeDtypeStruct((batch * seq, 512), BF16),
        scratch_shapes=[
            pltpu.VMEM((nc, DSA_ROWS, ck), I32),
            pltpu.VMEM((DSA_ROWS, LANES), I32),
            pltpu.VMEM((DSA_ROWS, LANES), I32),
            pltpu.VMEM((DSA_ROWS, ck), F32),
            pltpu.VMEM((DSA_HEADS, DSA_ROWS, 1), F32),
            pltpu.VMEM((DSA_HEADS, DSA_ROWS, LANES), F32),
        ],
        compiler_params=_cparams(("parallel", "arbitrary")),
        name="dsa",
    )(qb, qi, wi, chunked(ki), chunked(kb), chunked(vb))


def _out_proj_kernel(oa_ref, ob_ref, x_ref, wa_ref, wb_ref, g_ref, b_ref, h_ref, hb_ref):
    mix = jnp.dot(oa_ref[...], wa_ref[...], preferred_element_type=F32)
    mix = mix + jnp.dot(ob_ref[...], wb_ref[...], preferred_element_type=F32)
    h = _layer_norm_rows(ALPHA * x_ref[...] + mix, g_ref[...], b_ref[...])
    h_ref[...] = h
    hb_ref[...] = h.astype(BF16)


def _out_proj(oa, ob, x2, wo_a, wo_b, g, b, tm):
    n = x2.shape[0]
    row = lambda w: pl.BlockSpec((tm, w), lambda i: (i, 0))
    const = lambda r, c: pl.BlockSpec((r, c), lambda i: (0, 0))
    return pl.pallas_call(
        _out_proj_kernel,
        grid=(n // tm,),
        in_specs=[row(512), row(512), row(D_MODEL), const(512, D_MODEL), const(512, D_MODEL),
                  const(1, D_MODEL), const(1, D_MODEL)],
        out_specs=[row(D_MODEL), row(D_MODEL)],
        out_shape=[jax.ShapeDtypeStruct((n, D_MODEL), F32), jax.ShapeDtypeStruct((n, D_MODEL), BF16)],
        compiler_params=_cparams(("parallel",)),
        name="out_proj",
    )(oa, ob, x2, wo_a, wo_b, g, b)


def _top_values(s, count, with_rank):
    vals = []
    rank = jnp.full(s.shape, float(count), F32) if with_rank else None
    cur = s
    for j in range(count):
        m = jnp.max(cur, axis=0, keepdims=True)
        vals.append(m)
        below = cur < m
        cur = jnp.where(below, cur, -jnp.inf)
        if with_rank:
            rank = jnp.where(below, rank, jnp.minimum(rank, float(j)))
    return vals, rank


def _peer_route_kernel(h_ref, wqt_ref, sk_ref, r2_ref, p2_ref, na_ref, p1_ref):
    qt = lax.dot_general(wqt_ref[...], h_ref[...], _NT, preferred_element_type=F32)
    half = PEER_DKEY // 2
    for h in range(PEER_HEADS):
        s = []
        for p in range(2):
            idx = h * 2 + p
            qhp = qt[idx * half:(idx + 1) * half, :].astype(BF16)
            s.append(jnp.dot(sk_ref[idx], qhp, preferred_element_type=F32))
        s1, s2 = s
        v1, _ = _top_values(s1, PEER_TOPK, False)
        v2, rank2 = _top_values(s2, PEER_TOPK, True)
        v2_all = jnp.concatenate(v2, axis=0)
        sub = lax.broadcasted_iota(I32, (8, v2_all.shape[1]), 0)
        cands = [v1[0] + v2_all]
        for i in range(1, 8):
            lim = PEER_TOPK // (i + 1)
            cands.append(jnp.where(sub < lim, v1[i] + v2_all[:8], -jnp.inf))
        cands.append(jnp.concatenate(v1[8:], axis=0) + v2[0])
        cand = jnp.concatenate(cands, axis=0)
        tops, _ = _top_values(cand, PEER_TOPK, False)
        tau = tops[-1]
        tau = jnp.where(tau > -jnp.inf, tau, jnp.min(jnp.where(cand > -jnp.inf, cand, jnp.inf), axis=0,
                                                     keepdims=True))
        mx = v1[0] + v2[0]
        z = jnp.sum(jnp.where(cand >= tau, jnp.exp(cand - mx), 0.0), axis=0, keepdims=True)
        na = jnp.zeros(s1.shape, F32)
        for j in range(8):
            na = na + jnp.where((s1 + v2[j]) >= tau, 1.0, 0.0)
        n_top = jnp.sum(jnp.where((v1[0] + v2_all[8:]) >= tau, 1.0, 0.0), axis=0, keepdims=True)
        na = na + jnp.where(s1 == v1[0], n_top, 0.0)
        r2_ref[h] = pltpu.bitcast(rank2.astype(BF16), jnp.uint32)
        p2_ref[h] = pltpu.bitcast(jnp.exp(s2 - v2[0]).astype(BF16), jnp.uint32)
        na_ref[h] = na
        p1_ref[h] = jnp.exp(s1 - v1[0]) / z


def _peer_route(hb, wqt, sk, tb):
    n = hb.shape[0]
    word_rows = PEER_NKEYS * jnp.dtype(BF16).itemsize // 4
    tab16 = jax.ShapeDtypeStruct((PEER_HEADS, word_rows, n), jnp.uint32)
    tab32 = jax.ShapeDtypeStruct((PEER_HEADS, PEER_NKEYS, n), F32)
    spec16 = pl.BlockSpec((PEER_HEADS, word_rows, tb), lambda i: (0, 0, i))
    spec32 = pl.BlockSpec((PEER_HEADS, PEER_NKEYS, tb), lambda i: (0, 0, i))
    return pl.pallas_call(
        _peer_route_kernel,
        grid=(n // tb,),
        in_specs=[pl.BlockSpec((tb, D_MODEL), lambda i: (i, 0)),
                  pl.BlockSpec(wqt.shape, lambda i: (0, 0)),
                  pl.BlockSpec(sk.shape, lambda i: (0, 0, 0))],
        out_specs=[spec16, spec16, spec32, spec32],
        out_shape=[tab16, tab16, tab32, tab32],
        compiler_params=_cparams(("parallel",)),
        name="peer_route",
    )(hb, wqt, sk)


def _peer_ffn_kernel(hb_ref, h_ref, u_ref, vt_ref, r2_ref, p2_ref, na_ref, p1_ref, g_ref, b_ref,
                     o_ref, acc_ref, w_ref, *, tb, eb, sub):
    j = pl.program_id(1)

    @pl.when(j == 0)
    def _():
        acc_ref[...] = jnp.zeros(acc_ref.shape, F32)

    groups = PEER_NKEYS // BF16_ROWS
    wr = r2_ref.shape[1] // groups
    zero = jnp.zeros((), BF16)
    for sb in range(eb // sub):
        srows = slice(sb * sub, (sb + 1) * sub)
        act_t = lax.dot_general(u_ref[srows, :], hb_ref[...], _NT, preferred_element_type=F32)
        for asub in range(sub // PEER_NKEYS):
            al = sb * (sub // PEER_NKEYS) + asub
            for tt in range(tb // LANES):
                cols = slice(tt * LANES, (tt + 1) * LANES)
                gate = [jnp.zeros((BF16_ROWS, LANES), BF16) for _ in range(groups)]
                for h in range(PEER_HEADS):
                    na = jnp.broadcast_to(na_ref[h, al:al + 1, cols], (BF16_ROWS, LANES)).astype(BF16)
                    p1 = jnp.broadcast_to(p1_ref[h, al:al + 1, cols], (BF16_ROWS, LANES)).astype(BF16)
                    for g in range(groups):
                        ws = slice(g * wr, (g + 1) * wr)
                        r2 = pltpu.bitcast(r2_ref[h, ws, cols], BF16)
                        p2 = pltpu.bitcast(p2_ref[h, ws, cols], BF16)
                        gate[g] = gate[g] + jnp.where(r2 < na, p2, zero) * p1
                for g in range(groups):
                    lrows = slice(asub * PEER_NKEYS + g * BF16_ROWS, asub * PEER_NKEYS + (g + 1) * BF16_ROWS)
                    x = act_t[lrows, cols]
                    gelu = 0.5 * x * (1.0 + lax.erf(x * (2.0 ** -0.5)))
                    w_ref[sb, lrows, cols] = gelu.astype(BF16) * gate[g]
    w_all = w_ref[...].reshape(eb, tb)
    acc_ref[...] += jnp.dot(vt_ref[...], w_all, preferred_element_type=F32)

    @pl.when(j == pl.num_programs(1) - 1)
    def _():
        z = ALPHA * h_ref[...] + acc_ref[...].T
        o_ref[...] = _layer_norm_rows(z, g_ref[...], b_ref[...])


def _peer_ffn(hb, h, u, vt, tabs, g, b, tb, eb):
    n = hb.shape[0]
    ne = u.shape[0]
    tab_spec = pl.BlockSpec((PEER_HEADS, tabs[0].shape[1], tb), lambda i, j: (0, 0, i))
    a_spec = pl.BlockSpec((PEER_HEADS, eb // PEER_NKEYS, tb), lambda i, j: (0, j, i))
    sub = 512
    kernel = functools.partial(_peer_ffn_kernel, tb=tb, eb=eb, sub=sub)
    return pl.pallas_call(
        kernel,
        grid=(n // tb, ne // eb),
        in_specs=[pl.BlockSpec((tb, D_MODEL), lambda i, j: (i, 0)),
                  pl.BlockSpec((tb, D_MODEL), lambda i, j: (i, 0)),
                  pl.BlockSpec((eb, D_MODEL), lambda i, j: (j, 0)),
                  pl.BlockSpec((D_MODEL, eb), lambda i, j: (0, j)),
                  tab_spec, tab_spec, a_spec, a_spec,
                  pl.BlockSpec((1, D_MODEL), lambda i, j: (0, 0)),
                  pl.BlockSpec((1, D_MODEL), lambda i, j: (0, 0))],
        out_specs=pl.BlockSpec((tb, D_MODEL), lambda i, j: (i, 0)),
        out_shape=jax.ShapeDtypeStruct((n, D_MODEL), F32),
        scratch_shapes=[pltpu.VMEM((D_MODEL, tb), F32), pltpu.VMEM((eb // sub, sub, tb), BF16)],
        compiler_params=_cparams(("parallel", "arbitrary")),
        name="peer_ffn",
    )(hb, h, u, vt, *tabs, g, b)


def _tiles(n_tokens, seq):
    tm = 256 if seq % 256 == 0 else BLOCK
    ck = 512 if seq >= 2048 else BLOCK
    tb_route = 256 if n_tokens % 256 == 0 else BLOCK
    tb_ffn = 512 if n_tokens % 512 == 0 else BLOCK
    eb = 2048
    return tm, ck, tb_route, tb_ffn, eb


def kernel(x, w_in, idx_k_g, idx_k_b, sinks, w_o, ln1_g, ln1_b, peer_wq, peer_subkeys, peer_u, peer_v,
           ln2_g, ln2_b):
    batch, seq, _ = x.shape
    n = batch * seq
    tm, ck, tb_route, tb_ffn, eb = _tiles(n, seq)

    inv = ROPE_THETA ** (-jnp.arange(0, HEAD_DIM, 2, dtype=F32) / HEAD_DIM)
    ang = jnp.arange(seq, dtype=F32)[:, None] * inv[None, :]
    cos, sin = jnp.cos(ang), jnp.sin(ang)
    cos_t = jnp.concatenate([cos, cos, cos, cos], axis=-1)
    sin_t = jnp.concatenate([-sin, sin, -sin, sin], axis=-1)

    h = x.reshape(n, D_MODEL)
    for l in range(DEPTH):
        w_pad = jnp.pad(w_in[l], ((0, 0), (0, MIX_COLS_PAD - MIX_COLS))).astype(BF16)
        kg = jnp.pad(idx_k_g[l], (0, LANES - IDX_DIM)).reshape(1, LANES)
        kb = jnp.pad(idx_k_b[l], (0, LANES - IDX_DIM)).reshape(1, LANES)
        qa, ka, va, qb, kbb, vb, qi, ki, wi = _in_proj(h, w_pad, cos_t, sin_t, kg, kb, seq, tm)
        oa = _swa(sinks[l], qa, ka, va, batch, seq)
        ob = _dsa(qb, qi, wi, ki, kbb, vb, batch, seq, ck)
        wo = w_o[l].astype(BF16)
        h1, h1b = _out_proj(oa, ob, h, wo[:512], wo[512:], ln1_g[l].reshape(1, -1), ln1_b[l].reshape(1, -1), tm)
        wqt = peer_wq[l].T.astype(BF16)
        sk = peer_subkeys[l].reshape(PEER_HEADS * 2, PEER_NKEYS, PEER_DKEY // 2).astype(BF16)
        tabs = _peer_route(h1b, wqt, sk, tb_route)
        u = peer_u[l].astype(BF16)
        vt = peer_v[l].T.astype(BF16)
        h = _peer_ffn(h1b, h1, u, vt, tabs, ln2_g[l].reshape(1, -1), ln2_b[l].reshape(1, -1), tb_ffn, eb)
    return h.reshape(batch, seq, D_MODEL)
```

```python
import functools
import math

import jax
import jax.numpy as jnp
from jax import lax
from jax.experimental import pallas as pl
from jax.experimental.pallas import tpu as pltpu

F32 = jnp.float32
BF16 = jnp.bfloat16
I32 = jnp.int32

D_MODEL = 1024
HEAD_DIM = 64
SWA_HEADS = 8
SWA_KV_HEADS = 2
WINDOW = 128
BLOCK = 128
DSA_HEADS = 8
IDX_HEADS = 4
IDX_DIM = 64
TOPK_MAX = 256
ROPE_THETA = 10000.0
PEER_HEADS = 8
PEER_NKEYS = 128
PEER_DKEY = 256
PEER_TOPK = 16
LN_EPS = 1e-5
DEPTH = 1
ALPHA = (2.0 * DEPTH) ** 0.25

LANES = 128
BF16_ROWS = 16
VMEM_LIMIT_BYTES = 56 * 1024 * 1024

MIX_COLS = 1732
MIX_COLS_PAD = 1792
WI_LANE0 = 64

NEG_BIG = -1e30
INT_MIN = -(2 ** 31)

_NT = (((1,), (1,)), ((), ()))


def _cparams(sem):
    return pltpu.CompilerParams(dimension_semantics=sem, vmem_limit_bytes=VMEM_LIMIT_BYTES)


def _layer_norm_rows(z, g, b):
    mu = jnp.mean(z, axis=-1, keepdims=True)
    zc = z - mu
    var = jnp.mean(zc * zc, axis=-1, keepdims=True)
    return zc * lax.rsqrt(var + LN_EPS) * g + b


def _in_proj_kernel(x_ref, w_ref, cos_ref, sin_ref, kg_ref, kb_ref,
                    qa_ref, ka_ref, va_ref, qb_ref, kb_out_ref, vb_ref, qi_ref, ki_ref, wi_ref):
    xb = x_ref[...].astype(BF16)
    proj = jnp.dot(xb, w_ref[...], preferred_element_type=F32)
    cos = cos_ref[...]
    sin = sin_ref[...]
    lane = lax.broadcasted_iota(I32, cos.shape, 1)
    first_half = (lane % HEAD_DIM) < (HEAD_DIM // 2)

    def rope(t):
        swapped = jnp.where(first_half, pltpu.roll(t, LANES - HEAD_DIM // 2, 1),
                            pltpu.roll(t, HEAD_DIM // 2, 1))
        return t * cos + swapped * sin

    def slab(i):
        return proj[:, i * LANES:(i + 1) * LANES]

    q_scale = HEAD_DIM ** -0.5
    for i in range(4):
        qa_ref[:, i * LANES:(i + 1) * LANES] = (rope(slab(i)) * q_scale).astype(BF16)
        qb_ref[:, i * LANES:(i + 1) * LANES] = (rope(slab(6 + i)) * q_scale).astype(BF16)
    ka_ref[...] = rope(slab(4)).astype(BF16)
    va_ref[...] = slab(5).astype(BF16)
    kv = slab(10)
    kv_r = rope(kv)
    kb_out_ref[...] = kv_r[:, :HEAD_DIM].astype(BF16)
    vb_ref[...] = jnp.where(lane < HEAD_DIM, pltpu.roll(kv, HEAD_DIM, 1),
                            jnp.where(lane == HEAD_DIM, 1.0, 0.0)).astype(BF16)
    for i in range(2):
        qi_ref[:, i * LANES:(i + 1) * LANES] = rope(slab(11 + i)).astype(BF16)
    last = slab(13)
    is_ki = lane < IDX_DIM
    mu = jnp.sum(jnp.where(is_ki, last, 0.0), axis=-1, keepdims=True) * (1.0 / IDX_DIM)
    cen = last - mu
    var = jnp.sum(jnp.where(is_ki, cen * cen, 0.0), axis=-1, keepdims=True) * (1.0 / IDX_DIM)
    kin = cen * lax.rsqrt(var + LN_EPS) * kg_ref[...] + kb_ref[...]
    kin = jnp.where(is_ki, kin, 0.0)
    ki_ref[...] = rope(kin)[:, :IDX_DIM].astype(BF16)
    w_scale = (IDX_HEADS ** -0.5) * (IDX_DIM ** -0.5)
    wi_ref[...] = last * w_scale


def _in_proj(x2, w_pad, cos_t, sin_t, kg, kb, seq, tm):
    n = x2.shape[0]
    nt = n // tm
    tps = seq // tm
    row = lambda w: pl.BlockSpec((tm, w), lambda i: (i, 0))
    out_shapes = [
        jax.ShapeDtypeStruct((n, 512), BF16), jax.ShapeDtypeStruct((n, 128), BF16),
        jax.ShapeDtypeStruct((n, 128), BF16), jax.ShapeDtypeStruct((n, 512), BF16),
        jax.ShapeDtypeStruct((n, 64), BF16), jax.ShapeDtypeStruct((n, 128), BF16),
        jax.ShapeDtypeStruct((n, 256), BF16), jax.ShapeDtypeStruct((n, 64), BF16),
        jax.ShapeDtypeStruct((n, 128), F32),
    ]
    return pl.pallas_call(
        _in_proj_kernel,
        grid=(nt,),
        in_specs=[
            row(D_MODEL),
            pl.BlockSpec((D_MODEL, MIX_COLS_PAD), lambda i: (0, 0)),
            pl.BlockSpec((tm, LANES), lambda i: (i % tps, 0)),
            pl.BlockSpec((tm, LANES), lambda i: (i % tps, 0)),
            pl.BlockSpec((1, LANES), lambda i: (0, 0)),
            pl.BlockSpec((1, LANES), lambda i: (0, 0)),
        ],
        out_specs=[row(512), row(128), row(128), row(512), row(64), row(128), row(256), row(64), row(128)],
        out_shape=out_shapes,
        compiler_params=_cparams(("parallel",)),
        name="in_proj",
    )(x2, w_pad, cos_t, sin_t, kg, kb)


def _swa_kernel(sink_ref, q_ref, kp_ref, kc_ref, vp_ref, vc_ref, o_ref):
    n = pl.program_id(1)
    qi = lax.broadcasted_iota(I32, (BLOCK, 2 * BLOCK), 0)
    sj = lax.broadcasted_iota(I32, (BLOCK, 2 * BLOCK), 1)
    diff = qi + BLOCK - sj
    valid = (diff >= 0) & (diff < WINDOW) & ((n > 0) | (sj >= BLOCK))
    group = SWA_HEADS // SWA_KV_HEADS
    for kh in range(SWA_KV_HEADS):
        cs = slice(kh * HEAD_DIM, (kh + 1) * HEAD_DIM)
        k = jnp.concatenate([kp_ref[:, cs], kc_ref[:, cs]], axis=0)
        v = jnp.concatenate([vp_ref[:, cs], vc_ref[:, cs]], axis=0)
        for g in range(group):
            h = kh * group + g
            hs = slice(h * HEAD_DIM, (h + 1) * HEAD_DIM)
            s = lax.dot_general(q_ref[:, hs], k, _NT, preferred_element_type=F32)
            s = jnp.where(valid, s, NEG_BIG)
            sink = sink_ref[h]
            m = jnp.maximum(jnp.max(s, axis=-1, keepdims=True), sink)
            p = jnp.exp(s - m)
            denom = jnp.sum(p, axis=-1, keepdims=True) + jnp.exp(sink - m)
            o = jnp.dot(p.astype(BF16), v, preferred_element_type=F32)
            o_ref[:, hs] = (o / denom).astype(o_ref.dtype)


def _swa(sinks, qa, ka, va, batch, seq):
    nb = seq // BLOCK
    cur = lambda w: pl.BlockSpec((BLOCK, w), lambda b, n: (b * nb + n, 0))
    prev = lambda w: pl.BlockSpec((BLOCK, w), lambda b, n: (b * nb + jnp.maximum(n - 1, 0), 0))
    return pl.pallas_call(
        _swa_kernel,
        grid=(batch, nb),
        in_specs=[pl.BlockSpec(memory_space=pltpu.SMEM), cur(512), prev(128), cur(128), prev(128), cur(128)],
        out_specs=cur(512),
        out_shape=jax.ShapeDtypeStruct((batch * seq, 512), BF16),
        compiler_params=_cparams(("parallel", "parallel")),
        name="swa",
    )(sinks, qa, ka, ka, va, va)


DSA_ROWS = 2 * BLOCK


def _dsa_kernel(qb_ref, qi_ref, wi_ref, ki_ref, kb_ref, vb_ref, o_ref,
                keys_ref, vst_ref, tie_ref, bias_ref, m_ref, acc_ref, qs_ref, s_ref, p_ref, alpha_ref,
                *, ck, nc, k_sel, seq_bits):
    n = pl.program_id(1)
    t0 = n * DSA_ROWS
    nck = (t0 + DSA_ROWS + ck - 1) // ck
    lane_tiles = ck // LANES
    qpos = t0 + lax.broadcasted_iota(I32, (DSA_ROWS, ck), 0)
    lane_pos = lax.broadcasted_iota(I32, (DSA_ROWS, ck), 1)
    tile_lane = lax.broadcasted_iota(I32, (BLOCK, LANES), 1)
    ones = jnp.ones((LANES, LANES), BF16)

    qi = qi_ref[...]
    qi4 = jnp.concatenate([qi[:, h * IDX_DIM:(h + 1) * IDX_DIM] for h in range(IDX_HEADS)], axis=0)
    wi = wi_ref[...]
    wcols = [wi[:, WI_LANE0 + h:WI_LANE0 + h + 1] for h in range(IDX_HEADS)]

    def score_chunk(c, carry):
        r = lax.dot_general(qi4, ki_ref[c], _NT, preferred_element_type=F32)
        r = jnp.maximum(r, 0.0)
        isc = wcols[0] * r[0:DSA_ROWS]
        for h in range(1, IDX_HEADS):
            isc = isc + wcols[h] * r[h * DSA_ROWS:(h + 1) * DSA_ROWS]
        isc = jnp.where(isc == 0.0, 0.0, isc)
        bits = pltpu.bitcast(isc, I32)
        key = bits ^ ((bits >> 31) & 0x7FFFFFFF)
        keys_ref[c] = jnp.where(c * ck + lane_pos <= qpos, key, INT_MIN)
        return carry

    lax.fori_loop(0, nck, score_chunk, 0)

    def row_total(acc):
        return jnp.dot(acc.astype(BF16), ones, preferred_element_type=F32)

    def select(v):
        def count(grp, pred):
            acc = jnp.zeros((BLOCK, LANES), F32)
            for c in range(v):
                for j in range(lane_tiles):
                    key = keys_ref[c, grp * BLOCK:(grp + 1) * BLOCK, j * LANES:(j + 1) * LANES]
                    acc = acc + jnp.where(pred(key, c * ck + j * LANES + tile_lane), 1.0, 0.0)
            return acc

        def bit_step(i, carry):
            ta, tb, cand_b_prev, acc_b = carry
            bit = lax.shift_left(jnp.int32(1), 31 - i)
            tot_b = row_total(acc_b)
            cand_a = ta | bit
            cand_a_s = cand_a ^ INT_MIN
            acc_a = count(0, lambda key, pos: key >= cand_a_s)
            tb = jnp.where(tot_b >= k_sel, cand_b_prev, tb)
            cand_b = tb | bit
            cand_b_s = cand_b ^ INT_MIN
            acc_b = count(1, lambda key, pos: key >= cand_b_s)
            ta = jnp.where(row_total(acc_a) >= k_sel, cand_a, ta)
            return ta, tb, cand_b, acc_b

        zi = jnp.zeros((BLOCK, LANES), I32)
        ta, tb, cand_b, acc_b = lax.fori_loop(0, 32, bit_step, (zi, zi, zi, jnp.zeros((BLOCK, LANES), F32)))
        tb = jnp.where(row_total(acc_b) >= k_sel, cand_b, tb)
        vstar = (ta ^ INT_MIN, tb ^ INT_MIN)
        big = jnp.full((BLOCK, LANES), 2 ** seq_bits, I32)
        need, surplus = [], []
        for grp in range(2):
            vs = vstar[grp]
            vst_ref[grp * BLOCK:(grp + 1) * BLOCK, :] = vs
            tie_ref[grp * BLOCK:(grp + 1) * BLOCK, :] = big
            cnt_gt = row_total(count(grp, lambda key, pos: key > vs))
            cnt_eq = row_total(count(grp, lambda key, pos: key == vs))
            need.append(k_sel - cnt_gt)
            surplus.append(jnp.max(cnt_eq - need[grp]))

        @pl.when(jnp.maximum(surplus[0], surplus[1]) > 0.0)
        def _():
            for grp in range(2):
                vs = vstar[grp]

                def idx_step(i, p):
                    cand = p | lax.shift_left(jnp.int32(1), seq_bits - 1 - i)
                    cnt = row_total(count(grp, lambda key, pos: (key == vs) & (pos < cand)))
                    return jnp.where(cnt < need[grp], cand, p)

                tie_ref[grp * BLOCK:(grp + 1) * BLOCK, :] = lax.fori_loop(0, seq_bits, idx_step, zi)

    for v in range(1, nc + 1):
        pl.when(nck == v)(functools.partial(select, v))

    qb = qb_ref[...]
    for h in range(DSA_HEADS):
        qs_ref[h * DSA_ROWS:(h + 1) * DSA_ROWS, :] = qb[:, h * HEAD_DIM:(h + 1) * HEAD_DIM]
    m_ref[...] = jnp.full(m_ref.shape, NEG_BIG, F32)
    acc_ref[...] = jnp.zeros(acc_ref.shape, F32)
    row_lane = lax.broadcasted_iota(I32, (DSA_ROWS, LANES), 1)

    def stage_a(it, par):
        ca = jnp.minimum(it, nck - 1)
        s_ref[par] = lax.dot_general(qs_ref[...], kb_ref[ca], _NT, preferred_element_type=F32)

    def stage_b(it, par):
        cb = jnp.minimum(it - 1, nck - 1)
        live_b = it <= nck
        vst = vst_ref[...]
        tie = tie_ref[...]
        for j in range(lane_tiles):
            key = keys_ref[cb, :, j * LANES:(j + 1) * LANES]
            pos = cb * ck + j * LANES + row_lane
            sel = ((key > vst) | ((key == vst) & (pos <= tie))) & (key > INT_MIN)
            bias_ref[:, j * LANES:(j + 1) * LANES] = jnp.where(sel, 0.0, NEG_BIG)
        sb = 1 - par
        for h in range(DSA_HEADS):
            rows = slice(h * DSA_ROWS, (h + 1) * DSA_ROWS)
            s = s_ref[sb, rows, :] + bias_ref[...]
            s_ref[sb, rows, :] = s
            part = s[:, 0:LANES]
            for j in range(1, lane_tiles):
                part = jnp.maximum(part, s[:, j * LANES:(j + 1) * LANES])
            m_old = m_ref[h]
            m_new = jnp.where(live_b, jnp.maximum(m_old, jnp.max(part, axis=-1, keepdims=True)), m_old)
            alpha_ref[sb, h] = jnp.exp(m_old - m_new)
            m_ref[h] = m_new
        for h in range(DSA_HEADS):
            rows = slice(h * DSA_ROWS, (h + 1) * DSA_ROWS)
            p_ref[sb, rows, :] = jnp.exp(s_ref[sb, rows, :] - m_ref[h]).astype(BF16)

    def stage_c(it, par):
        cc = jnp.maximum(it - 2, 0)
        pv = jnp.dot(p_ref[par], vb_ref[cc], preferred_element_type=F32)
        for h in range(DSA_HEADS):
            acc_ref[h] = alpha_ref[par, h] * acc_ref[h] + pv[h * DSA_ROWS:(h + 1) * DSA_ROWS]

    def attend_step(it, carry):
        for par in range(2):
            @pl.when(it % 2 == par)
            def _():
                stage_a(it, par)
                stage_b(it, par)
                stage_c(it, par)
        return carry

    p_ref[1] = jnp.zeros(p_ref.shape[1:], BF16)
    alpha_ref[1] = jnp.ones(alpha_ref.shape[1:], F32)
    stage_a(0, 0)
    lax.fori_loop(1, nck + 2, attend_step, 0)
    for h in range(DSA_HEADS):
        a = acc_ref[h]
        o_ref[:, h * HEAD_DIM:(h + 1) * HEAD_DIM] = (a[:, :HEAD_DIM] / a[:, HEAD_DIM:HEAD_DIM + 1]).astype(o_ref.dtype)


def _dsa(qb, qi, wi, ki, kb, vb, batch, seq, ck):
    nq = seq // DSA_ROWS
    nc = seq // ck
    k_sel = min(TOPK_MAX, seq // 4)
    seq_bits = int(math.log2(seq))
    assert 2 ** seq_bits == seq and k_sel <= ck and seq % DSA_ROWS == 0
    assert seq // LANES <= 256
    chunked = lambda a: a.reshape(batch, nc, ck, a.shape[-1])
    blk = lambda w: pl.BlockSpec((DSA_ROWS, w), lambda b, n: (b * nq + n, 0))
    whole = lambda w: pl.BlockSpec((None, nc, ck, w), lambda b, n: (b, 0, 0, 0))
    kernel = functools.partial(_dsa_kernel, ck=ck, nc=nc, k_sel=k_sel, seq_bits=seq_bits)
    return pl.pallas_call(
        kernel,
        grid=(batch, nq),
        in_specs=[blk(512), blk(256), blk(128), whole(IDX_DIM), whole(HEAD_DIM), whole(LANES)],
        out_specs=blk(512),
        out_shape=jax.ShapeDtypeStruct((batch * seq, 512), BF16),
        scratch_shapes=[
            pltpu.VMEM((nc, DSA_ROWS, ck), I32),
            pltpu.VMEM((DSA_ROWS, LANES), I32),
            pltpu.VMEM((DSA_ROWS, LANES), I32),
            pltpu.VMEM((DSA_ROWS, ck), F32),
            pltpu.VMEM((DSA_HEADS, DSA_ROWS, 1), F32),
            pltpu.VMEM((DSA_HEADS, DSA_ROWS, LANES), F32),
            pltpu.VMEM((DSA_HEADS * DSA_ROWS, HEAD_DIM), BF16),
            pltpu.VMEM((2, DSA_HEADS * DSA_ROWS, ck), F32),
            pltpu.VMEM((2, DSA_HEADS * DSA_ROWS, ck), BF16),
            pltpu.VMEM((2, DSA_HEADS, DSA_ROWS, 1), F32),
        ],
        compiler_params=_cparams(("parallel", "arbitrary")),
        name="dsa",
    )(qb, qi, wi, chunked(ki), chunked(kb), chunked(vb))


def _out_proj_kernel(oa_ref, ob_ref, x_ref, wa_ref, wb_ref, g_ref, b_ref, h_ref, hb_ref):
    mix = jnp.dot(oa_ref[...], wa_ref[...], preferred_element_type=F32)
    mix = mix + jnp.dot(ob_ref[...], wb_ref[...], preferred_element_type=F32)
    h = _layer_norm_rows(ALPHA * x_ref[...] + mix, g_ref[...], b_ref[...])
    h_ref[...] = h
    hb_ref[...] = h.astype(BF16)


def _out_proj(oa, ob, x2, wo_a, wo_b, g, b, tm):
    n = x2.shape[0]
    row = lambda w: pl.BlockSpec((tm, w), lambda i: (i, 0))
    const = lambda r, c: pl.BlockSpec((r, c), lambda i: (0, 0))
    return pl.pallas_call(
        _out_proj_kernel,
        grid=(n // tm,),
        in_specs=[row(512), row(512), row(D_MODEL), const(512, D_MODEL), const(512, D_MODEL),
                  const(1, D_MODEL), const(1, D_MODEL)],
        out_specs=[row(D_MODEL), row(D_MODEL)],
        out_shape=[jax.ShapeDtypeStruct((n, D_MODEL), F32), jax.ShapeDtypeStruct((n, D_MODEL), BF16)],
        compiler_params=_cparams(("parallel",)),
        name="out_proj",
    )(oa, ob, x2, wo_a, wo_b, g, b)


def _top_values(s, count, with_rank):
    vals = []
    rank = jnp.full(s.shape, float(count), F32) if with_rank else None
    cur = s
    for j in range(count):
        m = jnp.max(cur, axis=0, keepdims=True)
        vals.append(m)
        below = cur < m
        cur = jnp.where(below, cur, -jnp.inf)
        if with_rank:
            rank = jnp.where(below, rank, jnp.minimum(rank, float(j)))
    return vals, rank


def _peer_route_kernel(h_ref, wqt_ref, sk_ref, r2_ref, p2_ref, na_ref, p1_ref):
    qt = lax.dot_general(wqt_ref[...], h_ref[...], _NT, preferred_element_type=F32)
    half = PEER_DKEY // 2
    for h in range(PEER_HEADS):
        s = []
        for p in range(2):
            idx = h * 2 + p
            qhp = qt[idx * half:(idx + 1) * half, :].astype(BF16)
            s.append(jnp.dot(sk_ref[idx], qhp, preferred_element_type=F32))
        s1, s2 = s
        v1, _ = _top_values(s1, PEER_TOPK, False)
        v2, rank2 = _top_values(s2, PEER_TOPK, True)
        v2_all = jnp.concatenate(v2, axis=0)
        sub = lax.broadcasted_iota(I32, (8, v2_all.shape[1]), 0)
        cands = [v1[0] + v2_all]
        for i in range(1, 8):
            lim = PEER_TOPK // (i + 1)
            cands.append(jnp.where(sub < lim, v1[i] + v2_all[:8], -jnp.inf))
        cands.append(jnp.concatenate(v1[8:], axis=0) + v2[0])
        cand = jnp.concatenate(cands, axis=0)
        tops, _ = _top_values(cand, PEER_TOPK, False)
        tau = tops[-1]
        tau = jnp.where(tau > -jnp.inf, tau, jnp.min(jnp.where(cand > -jnp.inf, cand, jnp.inf), axis=0,
                                                     keepdims=True))
        mx = v1[0] + v2[0]
        z = jnp.sum(jnp.where(cand >= tau, jnp.exp(cand - mx), 0.0), axis=0, keepdims=True)
        na = jnp.zeros(s1.shape, F32)
        for j in range(8):
            na = na + jnp.where((s1 + v2[j]) >= tau, 1.0, 0.0)
        n_top = jnp.sum(jnp.where((v1[0] + v2_all[8:]) >= tau, 1.0, 0.0), axis=0, keepdims=True)
        na = na + jnp.where(s1 == v1[0], n_top, 0.0)
        r2_ref[h] = pltpu.bitcast(rank2.astype(BF16), jnp.uint32)
        p2_ref[h] = pltpu.bitcast(jnp.exp(s2 - v2[0]).astype(BF16), jnp.uint32)
        na_ref[h] = na
        p1_ref[h] = jnp.exp(s1 - v1[0]) / z


def _peer_route(hb, wqt, sk, tb):
    n = hb.shape[0]
    word_rows = PEER_NKEYS * jnp.dtype(BF16).itemsize // 4
    tab16 = jax.ShapeDtypeStruct((PEER_HEADS, word_rows, n), jnp.uint32)
    tab32 = jax.ShapeDtypeStruct((PEER_HEADS, PEER_NKEYS, n), F32)
    spec16 = pl.BlockSpec((PEER_HEADS, word_rows, tb), lambda i: (0, 0, i))
    spec32 = pl.BlockSpec((PEER_HEADS, PEER_NKEYS, tb), lambda i: (0, 0, i))
    return pl.pallas_call(
        _peer_route_kernel,
        grid=(n // tb,),
        in_specs=[pl.BlockSpec((tb, D_MODEL), lambda i: (i, 0)),
                  pl.BlockSpec(wqt.shape, lambda i: (0, 0)),
                  pl.BlockSpec(sk.shape, lambda i: (0, 0, 0))],
        out_specs=[spec16, spec16, spec32, spec32],
        out_shape=[tab16, tab16, tab32, tab32],
        compiler_params=_cparams(("parallel",)),
        name="peer_route",
    )(hb, wqt, sk)


def _peer_ffn_kernel(hb_ref, h_ref, u_ref, vt_ref, r2_ref, p2_ref, na_ref, p1_ref, g_ref, b_ref,
                     o_ref, acc_ref, w_ref, *, tb, eb, sub):
    j = pl.program_id(1)

    @pl.when(j == 0)
    def _():
        acc_ref[...] = jnp.zeros(acc_ref.shape, F32)

    groups = PEER_NKEYS // BF16_ROWS
    wr = r2_ref.shape[1] // groups
    zero = jnp.zeros((), BF16)
    for sb in range(eb // sub):
        srows = slice(sb * sub, (sb + 1) * sub)
        act_t = lax.dot_general(u_ref[srows, :], hb_ref[...], _NT, preferred_element_type=F32)
        for asub in range(sub // PEER_NKEYS):
            al = sb * (sub // PEER_NKEYS) + asub
            for tt in range(tb // LANES):
                cols = slice(tt * LANES, (tt + 1) * LANES)
                gate = [jnp.zeros((BF16_ROWS, LANES), BF16) for _ in range(groups)]
                for h in range(PEER_HEADS):
                    na = jnp.broadcast_to(na_ref[h, al:al + 1, cols], (BF16_ROWS, LANES)).astype(BF16)
                    p1 = jnp.broadcast_to(p1_ref[h, al:al + 1, cols], (BF16_ROWS, LANES)).astype(BF16)
                    for g in range(groups):
                        ws = slice(g * wr, (g + 1) * wr)
                        r2 = pltpu.bitcast(r2_ref[h, ws, cols], BF16)
                        p2 = pltpu.bitcast(p2_ref[h, ws, cols], BF16)
                        gate[g] = gate[g] + jnp.where(r2 < na, p2, zero) * p1
                for g in range(groups):
                    lrows = slice(asub * PEER_NKEYS + g * BF16_ROWS, asub * PEER_NKEYS + (g + 1) * BF16_ROWS)
                    x = act_t[lrows, cols]
                    gelu = 0.5 * x * (1.0 + lax.erf(x * (2.0 ** -0.5)))
                    w_ref[sb, lrows, cols] = gelu.astype(BF16) * gate[g]
    w_all = w_ref[...].reshape(eb, tb)
    acc_ref[...] += jnp.dot(vt_ref[...], w_all, preferred_element_type=F32)

    @pl.when(j == pl.num_programs(1) - 1)
    def _():
        z = ALPHA * h_ref[...] + acc_ref[...].T
        o_ref[...] = _layer_norm_rows(z, g_ref[...], b_ref[...])


def _peer_ffn(hb, h, u, vt, tabs, g, b, tb, eb):
    n = hb.shape[0]
    ne = u.shape[0]
    tab_spec = pl.BlockSpec((PEER_HEADS, tabs[0].shape[1], tb), lambda i, j: (0, 0, i))
    a_spec = pl.BlockSpec((PEER_HEADS, eb // PEER_NKEYS, tb), lambda i, j: (0, j, i))
    sub = 512
    kernel = functools.partial(_peer_ffn_kernel, tb=tb, eb=eb, sub=sub)
    return pl.pallas_call(
        kernel,
        grid=(n // tb, ne // eb),
        in_specs=[pl.BlockSpec((tb, D_MODEL), lambda i, j: (i, 0)),
                  pl.BlockSpec((tb, D_MODEL), lambda i, j: (i, 0)),
                  pl.BlockSpec((eb, D_MODEL), lambda i, j: (j, 0)),
                  pl.BlockSpec((D_MODEL, eb), lambda i, j: (0, j)),
                  tab_spec, tab_spec, a_spec, a_spec,
                  pl.BlockSpec((1, D_MODEL), lambda i, j: (0, 0)),
                  pl.BlockSpec((1, D_MODEL), lambda i, j: (0, 0))],
        out_specs=pl.BlockSpec((tb, D_MODEL), lambda i, j: (i, 0)),
        out_shape=jax.ShapeDtypeStruct((n, D_MODEL), F32),
        scratch_shapes=[pltpu.VMEM((D_MODEL, tb), F32), pltpu.VMEM((eb // sub, sub, tb), BF16)],
        compiler_params=_cparams(("parallel", "arbitrary")),
        name="peer_ffn",
    )(hb, h, u, vt, *tabs, g, b)


def _tiles(n_tokens, seq):
    tm = 256 if seq % 256 == 0 else BLOCK
    ck = 512 if seq >= 2048 else BLOCK
    tb_route = 256 if n_tokens % 256 == 0 else BLOCK
    tb_ffn = 512 if n_tokens % 512 == 0 else BLOCK
    eb = 2048
    return tm, ck, tb_route, tb_ffn, eb


def kernel(x, w_in, idx_k_g, idx_k_b, sinks, w_o, ln1_g, ln1_b, peer_wq, peer_subkeys, peer_u, peer_v,
           ln2_g, ln2_b):
    batch, seq, _ = x.shape
    n = batch * seq
    tm, ck, tb_route, tb_ffn, eb = _tiles(n, seq)

    inv = ROPE_THETA ** (-jnp.arange(0, HEAD_DIM, 2, dtype=F32) / HEAD_DIM)
    ang = jnp.arange(seq, dtype=F32)[:, None] * inv[None, :]
    cos, sin = jnp.cos(ang), jnp.sin(ang)
    cos_t = jnp.concatenate([cos, cos, cos, cos], axis=-1)
    sin_t = jnp.concatenate([-sin, sin, -sin, sin], axis=-1)

    h = x.reshape(n, D_MODEL)
    for l in range(DEPTH):
        w_pad = jnp.pad(w_in[l], ((0, 0), (0, MIX_COLS_PAD - MIX_COLS))).astype(BF16)
        kg = jnp.pad(idx_k_g[l], (0, LANES - IDX_DIM)).reshape(1, LANES)
        kb = jnp.pad(idx_k_b[l], (0, LANES - IDX_DIM)).reshape(1, LANES)
        qa, ka, va, qb, kbb, vb, qi, ki, wi = _in_proj(h, w_pad, cos_t, sin_t, kg, kb, seq, tm)
        oa = _swa(sinks[l], qa, ka, va, batch, seq)
        ob = _dsa(qb, qi, wi, ki, kbb, vb, batch, seq, ck)
        wo = w_o[l].astype(BF16)
        h1, h1b = _out_proj(oa, ob, h, wo[:512], wo[512:], ln1_g[l].reshape(1, -1), ln1_b[l].reshape(1, -1), tm)
        wqt = peer_wq[l].T.astype(BF16)
        sk = peer_subkeys[l].reshape(PEER_HEADS * 2, PEER_NKEYS, PEER_DKEY // 2).astype(BF16)
        tabs = _peer_route(h1b, wqt, sk, tb_route)
        u = peer_u[l].astype(BF16)
        vt = peer_v[l].T.astype(BF16)
        h = _peer_ffn(h1b, h1, u, vt, tabs, ln2_g[l].reshape(1, -1), ln2_b[l].reshape(1, -1), tb_ffn, eb)
    return h.reshape(batch, seq, D_MODEL)
```

```python
import functools
import math

import jax
import jax.numpy as jnp
from jax import lax
from jax.experimental import pallas as pl
from jax.experimental.pallas import tpu as pltpu

F32 = jnp.float32
BF16 = jnp.bfloat16
I32 = jnp.int32

D_MODEL = 1024
HEAD_DIM = 64
SWA_HEADS = 8
SWA_KV_HEADS = 2
WINDOW = 128
BLOCK = 128
DSA_HEADS = 8
IDX_HEADS = 4
IDX_DIM = 64
TOPK_MAX = 256
ROPE_THETA = 10000.0
PEER_HEADS = 8
PEER_NKEYS = 128
PEER_DKEY = 256
PEER_TOPK = 16
LN_EPS = 1e-5
DEPTH = 1
ALPHA = (2.0 * DEPTH) ** 0.25

LANES = 128
BF16_ROWS = 16
VMEM_LIMIT_BYTES = 56 * 1024 * 1024

MIX_COLS = 1732
MIX_COLS_PAD = 1792
WI_LANE0 = 64

NEG_BIG = -1e30
INT_MIN = -(2 ** 31)

_NT = (((1,), (1,)), ((), ()))


def _cparams(sem):
    return pltpu.CompilerParams(dimension_semantics=sem, vmem_limit_bytes=VMEM_LIMIT_BYTES)


def _layer_norm_rows(z, g, b):
    mu = jnp.mean(z, axis=-1, keepdims=True)
    zc = z - mu
    var = jnp.mean(zc * zc, axis=-1, keepdims=True)
    return zc * lax.rsqrt(var + LN_EPS) * g + b


def _in_proj_kernel(x_ref, w_ref, cos_ref, sin_ref, kg_ref, kb_ref,
                    qa_ref, ka_ref, va_ref, qb_ref, kb_out_ref, vb_ref, qi_ref, ki_ref, wi_ref):
    xb = x_ref[...].astype(BF16)
    proj = jnp.dot(xb, w_ref[...], preferred_element_type=F32)
    cos = cos_ref[...]
    sin = sin_ref[...]
    lane = lax.broadcasted_iota(I32, cos.shape, 1)
    first_half = (lane % HEAD_DIM) < (HEAD_DIM // 2)

    def rope(t):
        swapped = jnp.where(first_half, pltpu.roll(t, LANES - HEAD_DIM // 2, 1),
                            pltpu.roll(t, HEAD_DIM // 2, 1))
        return t * cos + swapped * sin

    def slab(i):
        return proj[:, i * LANES:(i + 1) * LANES]

    q_scale = HEAD_DIM ** -0.5
    for i in range(4):
        qa_ref[:, i * LANES:(i + 1) * LANES] = (rope(slab(i)) * q_scale).astype(BF16)
        qb_ref[:, i * LANES:(i + 1) * LANES] = (rope(slab(6 + i)) * q_scale).astype(BF16)
    ka_ref[...] = rope(slab(4)).astype(BF16)
    va_ref[...] = slab(5).astype(BF16)
    kv = slab(10)
    kv_r = rope(kv)
    kb_out_ref[...] = kv_r[:, :HEAD_DIM].astype(BF16)
    vb_ref[...] = jnp.where(lane < HEAD_DIM, pltpu.roll(kv, HEAD_DIM, 1),
                            jnp.where(lane == HEAD_DIM, 1.0, 0.0)).astype(BF16)
    for i in range(2):
        qi_ref[:, i * LANES:(i + 1) * LANES] = rope(slab(11 + i)).astype(BF16)
    last = slab(13)
    is_ki = lane < IDX_DIM
    mu = jnp.sum(jnp.where(is_ki, last, 0.0), axis=-1, keepdims=True) * (1.0 / IDX_DIM)
    cen = last - mu
    var = jnp.sum(jnp.where(is_ki, cen * cen, 0.0), axis=-1, keepdims=True) * (1.0 / IDX_DIM)
    kin = cen * lax.rsqrt(var + LN_EPS) * kg_ref[...] + kb_ref[...]
    kin = jnp.where(is_ki, kin, 0.0)
    ki_ref[...] = rope(kin)[:, :IDX_DIM].astype(BF16)
    w_scale = (IDX_HEADS ** -0.5) * (IDX_DIM ** -0.5)
    wi_ref[...] = last * w_scale


def _in_proj(x2, w_pad, cos_t, sin_t, kg, kb, seq, tm):
    n = x2.shape[0]
    nt = n // tm
    tps = seq // tm
    row = lambda w: pl.BlockSpec((tm, w), lambda i: (i, 0))
    out_shapes = [
        jax.ShapeDtypeStruct((n, 512), BF16), jax.ShapeDtypeStruct((n, 128), BF16),
        jax.ShapeDtypeStruct((n, 128), BF16), jax.ShapeDtypeStruct((n, 512), BF16),
        jax.ShapeDtypeStruct((n, 64), BF16), jax.ShapeDtypeStruct((n, 128), BF16),
        jax.ShapeDtypeStruct((n, 256), BF16), jax.ShapeDtypeStruct((n, 64), BF16),
        jax.ShapeDtypeStruct((n, 128), F32),
    ]
    return pl.pallas_call(
        _in_proj_kernel,
        grid=(nt,),
        in_specs=[
            row(D_MODEL),
            pl.BlockSpec((D_MODEL, MIX_COLS_PAD), lambda i: (0, 0)),
            pl.BlockSpec((tm, LANES), lambda i: (i % tps, 0)),
            pl.BlockSpec((tm, LANES), lambda i: (i % tps, 0)),
            pl.BlockSpec((1, LANES), lambda i: (0, 0)),
            pl.BlockSpec((1, LANES), lambda i: (0, 0)),
        ],
        out_specs=[row(512), row(128), row(128), row(512), row(64), row(128), row(256), row(64), row(128)],
        out_shape=out_shapes,
        compiler_params=_cparams(("parallel",)),
        name="in_proj",
    )(x2, w_pad, cos_t, sin_t, kg, kb)


def _swa_kernel(sink_ref, q_ref, kp_ref, kc_ref, vp_ref, vc_ref, o_ref):
    n = pl.program_id(1)
    qi = lax.broadcasted_iota(I32, (BLOCK, 2 * BLOCK), 0)
    sj = lax.broadcasted_iota(I32, (BLOCK, 2 * BLOCK), 1)
    diff = qi + BLOCK - sj
    valid = (diff >= 0) & (diff < WINDOW) & ((n > 0) | (sj >= BLOCK))
    group = SWA_HEADS // SWA_KV_HEADS
    for kh in range(SWA_KV_HEADS):
        cs = slice(kh * HEAD_DIM, (kh + 1) * HEAD_DIM)
        k = jnp.concatenate([kp_ref[:, cs], kc_ref[:, cs]], axis=0)
        v = jnp.concatenate([vp_ref[:, cs], vc_ref[:, cs]], axis=0)
        for g in range(group):
            h = kh * group + g
            hs = slice(h * HEAD_DIM, (h + 1) * HEAD_DIM)
            s = lax.dot_general(q_ref[:, hs], k, _NT, preferred_element_type=F32)
            s = jnp.where(valid, s, NEG_BIG)
            sink = sink_ref[h]
            m = jnp.maximum(jnp.max(s, axis=-1, keepdims=True), sink)
            p = jnp.exp(s - m)
            denom = jnp.sum(p, axis=-1, keepdims=True) + jnp.exp(sink - m)
            o = jnp.dot(p.astype(BF16), v, preferred_element_type=F32)
            o_ref[:, hs] = (o / denom).astype(o_ref.dtype)


def _swa(sinks, qa, ka, va, batch, seq):
    nb = seq // BLOCK
    cur = lambda w: pl.BlockSpec((BLOCK, w), lambda b, n: (b * nb + n, 0))
    prev = lambda w: pl.BlockSpec((BLOCK, w), lambda b, n: (b * nb + jnp.maximum(n - 1, 0), 0))
    return pl.pallas_call(
        _swa_kernel,
        grid=(batch, nb),
        in_specs=[pl.BlockSpec(memory_space=pltpu.SMEM), cur(512), prev(128), cur(128), prev(128), cur(128)],
        out_specs=cur(512),
        out_shape=jax.ShapeDtypeStruct((batch * seq, 512), BF16),
        compiler_params=_cparams(("parallel", "parallel")),
        name="swa",
    )(sinks, qa, ka, ka, va, va)


DSA_ROWS = 2 * BLOCK
HI_BITS = 14


def _dsa_kernel(qb_ref, qi_ref, wi_ref, ki_ref, kb_ref, vb_ref, o_ref,
                keys_ref, hi_ref, vst_ref, tie_ref, bias_ref, m_ref, acc_ref, qs_ref, s_ref, p_ref, alpha_ref,
                *, ck, nc, k_sel, seq_bits):
    n = pl.program_id(1)
    t0 = n * DSA_ROWS
    nck = (t0 + DSA_ROWS + ck - 1) // ck
    lane_tiles = ck // LANES
    qpos = t0 + lax.broadcasted_iota(I32, (DSA_ROWS, ck), 0)
    lane_pos = lax.broadcasted_iota(I32, (DSA_ROWS, ck), 1)
    tile_lane = lax.broadcasted_iota(I32, (BLOCK, LANES), 1)
    ones = jnp.ones((LANES, LANES), BF16)

    qi = qi_ref[...]
    qi4 = jnp.concatenate([qi[:, h * IDX_DIM:(h + 1) * IDX_DIM] for h in range(IDX_HEADS)], axis=0)
    wi = wi_ref[...]
    wcols = [wi[:, WI_LANE0 + h:WI_LANE0 + h + 1] for h in range(IDX_HEADS)]

    def score_chunk(c, carry):
        r = lax.dot_general(qi4, ki_ref[c], _NT, preferred_element_type=F32)
        r = jnp.maximum(r, 0.0)
        isc = wcols[0] * r[0:DSA_ROWS]
        for h in range(1, IDX_HEADS):
            isc = isc + wcols[h] * r[h * DSA_ROWS:(h + 1) * DSA_ROWS]
        isc = jnp.where(isc == 0.0, 0.0, isc)
        bits = pltpu.bitcast(isc, I32)
        key = bits ^ ((bits >> 31) & 0x7FFFFFFF)
        key = jnp.where(c * ck + lane_pos <= qpos, key, INT_MIN)
        keys_ref[c] = key
        hi_ref[c] = pltpu.bitcast(hi_pattern(lax.shift_right_logical(key ^ INT_MIN, 32 - HI_BITS)),
                                  jnp.uint32)
        return carry

    def hi_pattern(d):
        return pltpu.bitcast(lax.shift_left(d + 128, 16), F32).astype(BF16)

    lax.fori_loop(0, nck, score_chunk, 0)

    def row_total(acc):
        return jnp.dot(acc.astype(BF16), ones, preferred_element_type=F32)

    def select(v):
        def count(grp, pred):
            acc = jnp.zeros((BLOCK, LANES), F32)
            for c in range(v):
                for j in range(lane_tiles):
                    key = keys_ref[c, grp * BLOCK:(grp + 1) * BLOCK, j * LANES:(j + 1) * LANES]
                    acc = acc + jnp.where(pred(key, c * ck + j * LANES + tile_lane), 1.0, 0.0)
            return acc

        def bit_step(i, carry):
            ta, tb, cand_b_prev, acc_b = carry
            bit = lax.shift_left(jnp.int32(1), 31 - i)
            tot_b = row_total(acc_b)
            cand_a = ta | bit
            cand_a_s = cand_a ^ INT_MIN
            acc_a = count(0, lambda key, pos: key >= cand_a_s)
            tb = jnp.where(tot_b >= k_sel, cand_b_prev, tb)
            cand_b = tb | bit
            cand_b_s = cand_b ^ INT_MIN
            acc_b = count(1, lambda key, pos: key >= cand_b_s)
            ta = jnp.where(row_total(acc_a) >= k_sel, cand_a, ta)
            return ta, tb, cand_b, acc_b

        hw = hi_ref.shape[1] // 2

        def count_hi(grp, cand):
            acc = jnp.zeros((BLOCK, LANES), BF16)
            for c in range(v):
                for j in range(lane_tiles):
                    hi = pltpu.bitcast(hi_ref[c, grp * hw:(grp + 1) * hw, j * LANES:(j + 1) * LANES], BF16)
                    acc = acc + jnp.where(hi >= cand, jnp.ones((), BF16), jnp.zeros((), BF16))
            return acc

        def hi_step(i, carry):
            ta, tb, cand_b_prev, acc_b = carry
            bit = lax.shift_left(jnp.int32(1), HI_BITS - 1 - i)
            tot_b = row_total(acc_b)
            cand_a = ta | bit
            acc_a = count_hi(0, hi_pattern(cand_a))
            tb = jnp.where(tot_b >= k_sel, cand_b_prev, tb)
            cand_b = tb | bit
            acc_b = count_hi(1, hi_pattern(cand_b))
            ta = jnp.where(row_total(acc_a) >= k_sel, cand_a, ta)
            return ta, tb, cand_b, acc_b

        zi = jnp.zeros((BLOCK, LANES), I32)
        ta, tb, cand_b, acc_b = lax.fori_loop(0, HI_BITS, hi_step, (zi, zi, zi, jnp.zeros((BLOCK, LANES), BF16)))
        tb = jnp.where(row_total(acc_b) >= k_sel, cand_b, tb)
        ta = lax.shift_left(ta, 32 - HI_BITS)
        tb = lax.shift_left(tb, 32 - HI_BITS)
        ta, tb, cand_b, acc_b = lax.fori_loop(HI_BITS, 32, bit_step,
                                              (ta, tb, zi, jnp.zeros((BLOCK, LANES), F32)))
        tb = jnp.where(row_total(acc_b) >= k_sel, cand_b, tb)
        vstar = (ta ^ INT_MIN, tb ^ INT_MIN)
        big = jnp.full((BLOCK, LANES), 2 ** seq_bits, I32)
        need, surplus = [], []
        for grp in range(2):
            vs = vstar[grp]
            vst_ref[grp * BLOCK:(grp + 1) * BLOCK, :] = vs
            tie_ref[grp * BLOCK:(grp + 1) * BLOCK, :] = big
            cnt_gt = row_total(count(grp, lambda key, pos: key > vs))
            cnt_eq = row_total(count(grp, lambda key, pos: key == vs))
            need.append(k_sel - cnt_gt)
            surplus.append(jnp.max(cnt_eq - need[grp]))

        @pl.when(jnp.maximum(surplus[0], surplus[1]) > 0.0)
        def _():
            for grp in range(2):
                vs = vstar[grp]

                def idx_step(i, p):
                    cand = p | lax.shift_left(jnp.int32(1), seq_bits - 1 - i)
                    cnt = row_total(count(grp, lambda key, pos: (key == vs) & (pos < cand)))
                    return jnp.where(cnt < need[grp], cand, p)

                tie_ref[grp * BLOCK:(grp + 1) * BLOCK, :] = lax.fori_loop(0, seq_bits, idx_step, zi)

    for v in range(1, nc + 1):
        pl.when(nck == v)(functools.partial(select, v))

    qb = qb_ref[...]
    for h in range(DSA_HEADS):
        qs_ref[h * DSA_ROWS:(h + 1) * DSA_ROWS, :] = qb[:, h * HEAD_DIM:(h + 1) * HEAD_DIM]
    m_ref[...] = jnp.full(m_ref.shape, NEG_BIG, F32)
    acc_ref[...] = jnp.zeros(acc_ref.shape, F32)
    row_lane = lax.broadcasted_iota(I32, (DSA_ROWS, LANES), 1)

    def stage_a(it, par):
        ca = jnp.minimum(it, nck - 1)
        s_ref[par] = lax.dot_general(qs_ref[...], kb_ref[ca], _NT, preferred_element_type=F32)

    def stage_b(it, par):
        cb = jnp.minimum(it - 1, nck - 1)
        live_b = it <= nck
        vst = vst_ref[...]
        tie = tie_ref[...]
        for j in range(lane_tiles):
            key = keys_ref[cb, :, j * LANES:(j + 1) * LANES]
            pos = cb * ck + j * LANES + row_lane
            sel = ((key > vst) | ((key == vst) & (pos <= tie))) & (key > INT_MIN)
            bias_ref[:, j * LANES:(j + 1) * LANES] = jnp.where(sel, 0.0, NEG_BIG)
        sb = 1 - par
        for h in range(DSA_HEADS):
            rows = slice(h * DSA_ROWS, (h + 1) * DSA_ROWS)
            s = s_ref[sb, rows, :] + bias_ref[...]
            s_ref[sb, rows, :] = s
            part = s[:, 0:LANES]
            for j in range(1, lane_tiles):
                part = jnp.maximum(part, s[:, j * LANES:(j + 1) * LANES])
            m_old = m_ref[h]
            m_new = jnp.where(live_b, jnp.maximum(m_old, jnp.max(part, axis=-1, keepdims=True)), m_old)
            alpha_ref[sb, h] = jnp.exp(m_old - m_new)
            m_ref[h] = m_new
        for h in range(DSA_HEADS):
            rows = slice(h * DSA_ROWS, (h + 1) * DSA_ROWS)
            p_ref[sb, rows, :] = jnp.exp(s_ref[sb, rows, :] - m_ref[h]).astype(BF16)

    def stage_c(it, par):
        cc = jnp.maximum(it - 2, 0)
        pv = jnp.dot(p_ref[par], vb_ref[cc], preferred_element_type=F32)
        for h in range(DSA_HEADS):
            acc_ref[h] = alpha_ref[par, h] * acc_ref[h] + pv[h * DSA_ROWS:(h + 1) * DSA_ROWS]

    def attend_step(it, carry):
        for par in range(2):
            @pl.when(it % 2 == par)
            def _():
                stage_a(it, par)
                stage_b(it, par)
                stage_c(it, par)
        return carry

    p_ref[1] = jnp.zeros(p_ref.shape[1:], BF16)
    alpha_ref[1] = jnp.ones(alpha_ref.shape[1:], F32)
    stage_a(0, 0)
    lax.fori_loop(1, nck + 2, attend_step, 0)
    for h in range(DSA_HEADS):
        a = acc_ref[h]
        o_ref[:, h * HEAD_DIM:(h + 1) * HEAD_DIM] = (a[:, :HEAD_DIM] / a[:, HEAD_DIM:HEAD_DIM + 1]).astype(o_ref.dtype)


def _dsa(qb, qi, wi, ki, kb, vb, batch, seq, ck):
    nq = seq // DSA_ROWS
    nc = seq // ck
    k_sel = min(TOPK_MAX, seq // 4)
    seq_bits = int(math.log2(seq))
    assert 2 ** seq_bits == seq and k_sel <= ck and seq % DSA_ROWS == 0
    assert seq // LANES <= 256
    chunked = lambda a: a.reshape(batch, nc, ck, a.shape[-1])
    blk = lambda w: pl.BlockSpec((DSA_ROWS, w), lambda b, n: (b * nq + n, 0))
    whole = lambda w: pl.BlockSpec((None, nc, ck, w), lambda b, n: (b, 0, 0, 0))
    kernel = functools.partial(_dsa_kernel, ck=ck, nc=nc, k_sel=k_sel, seq_bits=seq_bits)
    return pl.pallas_call(
        kernel,
        grid=(batch, nq),
        in_specs=[blk(512), blk(256), blk(128), whole(IDX_DIM), whole(HEAD_DIM), whole(LANES)],
        out_specs=blk(512),
        out_shape=jax.ShapeDtypeStruct((batch * seq, 512), BF16),
        scratch_shapes=[
            pltpu.VMEM((nc, DSA_ROWS, ck), I32),
            pltpu.VMEM((nc, DSA_ROWS * jnp.dtype(BF16).itemsize // 4, ck), jnp.uint32),
            pltpu.VMEM((DSA_ROWS, LANES), I32),
            pltpu.VMEM((DSA_ROWS, LANES), I32),
            pltpu.VMEM((DSA_ROWS, ck), F32),
            pltpu.VMEM((DSA_HEADS, DSA_ROWS, 1), F32),
            pltpu.VMEM((DSA_HEADS, DSA_ROWS, LANES), F32),
            pltpu.VMEM((DSA_HEADS * DSA_ROWS, HEAD_DIM), BF16),
            pltpu.VMEM((2, DSA_HEADS * DSA_ROWS, ck), F32),
            pltpu.VMEM((2, DSA_HEADS * DSA_ROWS, ck), BF16),
            pltpu.VMEM((2, DSA_HEADS, DSA_ROWS, 1), F32),
        ],
        compiler_params=_cparams(("parallel", "arbitrary")),
        name="dsa",
    )(qb, qi, wi, chunked(ki), chunked(kb), chunked(vb))


def _out_proj_kernel(oa_ref, ob_ref, x_ref, wa_ref, wb_ref, g_ref, b_ref, h_ref, hb_ref):
    mix = jnp.dot(oa_ref[...], wa_ref[...], preferred_element_type=F32)
    mix = mix + jnp.dot(ob_ref[...], wb_ref[...], preferred_element_type=F32)
    h = _layer_norm_rows(ALPHA * x_ref[...] + mix, g_ref[...], b_ref[...])
    h_ref[...] = h
    hb_ref[...] = h.astype(BF16)


def _out_proj(oa, ob, x2, wo_a, wo_b, g, b, tm):
    n = x2.shape[0]
    row = lambda w: pl.BlockSpec((tm, w), lambda i: (i, 0))
    const = lambda r, c: pl.BlockSpec((r, c), lambda i: (0, 0))
    return pl.pallas_call(
        _out_proj_kernel,
        grid=(n // tm,),
        in_specs=[row(512), row(512), row(D_MODEL), const(512, D_MODEL), const(512, D_MODEL),
                  const(1, D_MODEL), const(1, D_MODEL)],
        out_specs=[row(D_MODEL), row(D_MODEL)],
        out_shape=[jax.ShapeDtypeStruct((n, D_MODEL), F32), jax.ShapeDtypeStruct((n, D_MODEL), BF16)],
        compiler_params=_cparams(("parallel",)),
        name="out_proj",
    )(oa, ob, x2, wo_a, wo_b, g, b)


def _top_values(s, count, with_rank):
    vals = []
    rank = jnp.full(s.shape, float(count), F32) if with_rank else None
    cur = s
    for j in range(count):
        m = jnp.max(cur, axis=0, keepdims=True)
        vals.append(m)
        below = cur < m
        cur = jnp.where(below, cur, -jnp.inf)
        if with_rank:
            rank = jnp.where(below, rank, jnp.minimum(rank, float(j)))
    return vals, rank


SUBLANES = 8


def _oddeven_merge_sort_pairs(n):
    pairs = []

    def merge(lo, hi, r):
        step = r * 2
        if step < hi - lo:
            merge(lo, hi, step)
            merge(lo + r, hi, step)
            pairs.extend((i, i + r) for i in range(lo + r, hi - r, step))
        else:
            pairs.append((lo, lo + r))

    def sort(lo, hi):
        if hi - lo >= 1:
            mid = lo + (hi - lo) // 2
            sort(lo, mid)
            sort(mid + 1, hi)
            merge(lo, hi, 1)

    sort(0, n - 1)
    return pairs


def _sorted_top16(s):
    n = PEER_NKEYS // SUBLANES
    x = [s[i * SUBLANES:(i + 1) * SUBLANES, :] for i in range(n)]
    for i, j in _oddeven_merge_sort_pairs(n):
        x[i], x[j] = jnp.maximum(x[i], x[j]), jnp.minimum(x[i], x[j])
    for shift in (4, 2, 1):
        y = [jnp.maximum(x[i], pltpu.roll(x[n - 1 - i], shift, 0)) for i in range(n)]
        for stride in (8, 4, 2, 1):
            for i in range(n):
                if i & stride == 0:
                    y[i], y[i + stride] = jnp.maximum(y[i], y[i + stride]), jnp.minimum(y[i], y[i + stride])
        x = y
    return x


def _peer_route_kernel(h_ref, wqt_ref, sk_ref, r2_ref, p2_ref, na_ref, p1_ref):
    qt = lax.dot_general(wqt_ref[...], h_ref[...], _NT, preferred_element_type=F32)
    half = PEER_DKEY // 2
    for h in range(PEER_HEADS):
        s = []
        for p in range(2):
            idx = h * 2 + p
            qhp = qt[idx * half:(idx + 1) * half, :].astype(BF16)
            s.append(jnp.dot(sk_ref[idx], qhp, preferred_element_type=F32))
        s1, s2 = s
        v1 = [x[0:1] for x in _sorted_top16(s1)]
        top2 = _sorted_top16(s2)
        v2 = [x[0:1] for x in top2]
        rank2 = []
        for i in range(PEER_NKEYS // SUBLANES):
            blk = s2[i * SUBLANES:(i + 1) * SUBLANES, :]
            r = jnp.where(blk < top2[0], 1.0, 0.0)
            for j in range(1, PEER_TOPK):
                r = r + jnp.where(blk < top2[j], 1.0, 0.0)
            rank2.append(r)
        rank2 = jnp.concatenate(rank2, axis=0)
        v2_all = jnp.concatenate(v2, axis=0)
        sub = lax.broadcasted_iota(I32, (8, v2_all.shape[1]), 0)
        cands = [v1[0] + v2_all]
        for i in range(1, 8):
            lim = PEER_TOPK // (i + 1)
            cands.append(jnp.where(sub < lim, v1[i] + v2_all[:8], -jnp.inf))
        cands.append(jnp.concatenate(v1[8:], axis=0) + v2[0])
        cand = jnp.concatenate(cands, axis=0)
        tops, _ = _top_values(cand, PEER_TOPK, False)
        tau = tops[-1]
        tau = jnp.where(tau > -jnp.inf, tau, jnp.min(jnp.where(cand > -jnp.inf, cand, jnp.inf), axis=0,
                                                     keepdims=True))
        mx = v1[0] + v2[0]
        z = jnp.sum(jnp.where(cand >= tau, jnp.exp(cand - mx), 0.0), axis=0, keepdims=True)
        na = jnp.zeros(s1.shape, F32)
        for j in range(8):
            na = na + jnp.where((s1 + v2[j]) >= tau, 1.0, 0.0)
        n_top = jnp.sum(jnp.where((v1[0] + v2_all[8:]) >= tau, 1.0, 0.0), axis=0, keepdims=True)
        na = na + jnp.where(s1 == v1[0], n_top, 0.0)
        r2_ref[h] = pltpu.bitcast(rank2.astype(BF16), jnp.uint32)
        p2_ref[h] = pltpu.bitcast(jnp.exp(s2 - v2[0]).astype(BF16), jnp.uint32)
        na_ref[h] = na
        p1_ref[h] = jnp.exp(s1 - v1[0]) / z


def _peer_route(hb, wqt, sk, tb):
    n = hb.shape[0]
    word_rows = PEER_NKEYS * jnp.dtype(BF16).itemsize // 4
    tab16 = jax.ShapeDtypeStruct((PEER_HEADS, word_rows, n), jnp.uint32)
    tab32 = jax.ShapeDtypeStruct((PEER_HEADS, PEER_NKEYS, n), F32)
    spec16 = pl.BlockSpec((PEER_HEADS, word_rows, tb), lambda i: (0, 0, i))
    spec32 = pl.BlockSpec((PEER_HEADS, PEER_NKEYS, tb), lambda i: (0, 0, i))
    return pl.pallas_call(
        _peer_route_kernel,
        grid=(n // tb,),
        in_specs=[pl.BlockSpec((tb, D_MODEL), lambda i: (i, 0)),
                  pl.BlockSpec(wqt.shape, lambda i: (0, 0)),
                  pl.BlockSpec(sk.shape, lambda i: (0, 0, 0))],
        out_specs=[spec16, spec16, spec32, spec32],
        out_shape=[tab16, tab16, tab32, tab32],
        compiler_params=_cparams(("parallel",)),
        name="peer_route",
    )(hb, wqt, sk)


def _peer_ffn_kernel(hb_ref, h_ref, u_ref, vt_ref, r2_ref, p2_ref, na_ref, p1_ref, g_ref, b_ref,
                     o_ref, acc_ref, w_ref, *, tb, eb, sub):
    j = pl.program_id(1)

    @pl.when(j == 0)
    def _():
        acc_ref[...] = jnp.zeros(acc_ref.shape, F32)

    groups = PEER_NKEYS // BF16_ROWS
    wr = r2_ref.shape[1] // groups
    zero = jnp.zeros((), BF16)
    for sb in range(eb // sub):
        srows = slice(sb * sub, (sb + 1) * sub)
        act_t = lax.dot_general(u_ref[srows, :], hb_ref[...], _NT, preferred_element_type=F32)
        for asub in range(sub // PEER_NKEYS):
            al = sb * (sub // PEER_NKEYS) + asub
            for tt in range(tb // LANES):
                cols = slice(tt * LANES, (tt + 1) * LANES)
                gate = [jnp.zeros((BF16_ROWS, LANES), BF16) for _ in range(groups)]
                for h in range(PEER_HEADS):
                    na = jnp.broadcast_to(na_ref[h, al:al + 1, cols], (BF16_ROWS, LANES)).astype(BF16)
                    p1 = jnp.broadcast_to(p1_ref[h, al:al + 1, cols], (BF16_ROWS, LANES)).astype(BF16)
                    for g in range(groups):
                        ws = slice(g * wr, (g + 1) * wr)
                        r2 = pltpu.bitcast(r2_ref[h, ws, cols], BF16)
                        p2 = pltpu.bitcast(p2_ref[h, ws, cols], BF16)
                        gate[g] = gate[g] + jnp.where(r2 < na, p2, zero) * p1
                for g in range(groups):
                    lrows = slice(asub * PEER_NKEYS + g * BF16_ROWS, asub * PEER_NKEYS + (g + 1) * BF16_ROWS)
                    x = act_t[lrows, cols]
                    gelu = 0.5 * x * (1.0 + lax.erf(x * (2.0 ** -0.5)))
                    w_ref[sb, lrows, cols] = gelu.astype(BF16) * gate[g]
    w_all = w_ref[...].reshape(eb, tb)
    acc_ref[...] += jnp.dot(vt_ref[...], w_all, preferred_element_type=F32)

    @pl.when(j == pl.num_programs(1) - 1)
    def _():
        z = ALPHA * h_ref[...] + acc_ref[...].T
        o_ref[...] = _layer_norm_rows(z, g_ref[...], b_ref[...])


def _peer_ffn(hb, h, u, vt, tabs, g, b, tb, eb):
    n = hb.shape[0]
    ne = u.shape[0]
    tab_spec = pl.BlockSpec((PEER_HEADS, tabs[0].shape[1], tb), lambda i, j: (0, 0, i))
    a_spec = pl.BlockSpec((PEER_HEADS, eb // PEER_NKEYS, tb), lambda i, j: (0, j, i))
    sub = 512
    kernel = functools.partial(_peer_ffn_kernel, tb=tb, eb=eb, sub=sub)
    return pl.pallas_call(
        kernel,
        grid=(n // tb, ne // eb),
        in_specs=[pl.BlockSpec((tb, D_MODEL), lambda i, j: (i, 0)),
                  pl.BlockSpec((tb, D_MODEL), lambda i, j: (i, 0)),
                  pl.BlockSpec((eb, D_MODEL), lambda i, j: (j, 0)),
                  pl.BlockSpec((D_MODEL, eb), lambda i, j: (0, j)),
                  tab_spec, tab_spec, a_spec, a_spec,
                  pl.BlockSpec((1, D_MODEL), lambda i, j: (0, 0)),
                  pl.BlockSpec((1, D_MODEL), lambda i, j: (0, 0))],
        out_specs=pl.BlockSpec((tb, D_MODEL), lambda i, j: (i, 0)),
        out_shape=jax.ShapeDtypeStruct((n, D_MODEL), F32),
        scratch_shapes=[pltpu.VMEM((D_MODEL, tb), F32), pltpu.VMEM((eb // sub, sub, tb), BF16)],
        compiler_params=_cparams(("parallel", "arbitrary")),
        name="peer_ffn",
    )(hb, h, u, vt, *tabs, g, b)


def _tiles(n_tokens, seq):
    tm = 256 if seq % 256 == 0 else BLOCK
    ck = 512 if seq >= 2048 else BLOCK
    tb_route = 256 if n_tokens % 256 == 0 else BLOCK
    tb_ffn = 512 if n_tokens % 512 == 0 else BLOCK
    eb = 2048
    return tm, ck, tb_route, tb_ffn, eb


def kernel(x, w_in, idx_k_g, idx_k_b, sinks, w_o, ln1_g, ln1_b, peer_wq, peer_subkeys, peer_u, peer_v,
           ln2_g, ln2_b):
    batch, seq, _ = x.shape
    n = batch * seq
    tm, ck, tb_route, tb_ffn, eb = _tiles(n, seq)

    inv = ROPE_THETA ** (-jnp.arange(0, HEAD_DIM, 2, dtype=F32) / HEAD_DIM)
    ang = jnp.arange(seq, dtype=F32)[:, None] * inv[None, :]
    cos, sin = jnp.cos(ang), jnp.sin(ang)
    cos_t = jnp.concatenate([cos, cos, cos, cos], axis=-1)
    sin_t = jnp.concatenate([-sin, sin, -sin, sin], axis=-1)

    h = x.reshape(n, D_MODEL)
    for l in range(DEPTH):
        w_pad = jnp.pad(w_in[l], ((0, 0), (0, MIX_COLS_PAD - MIX_COLS))).astype(BF16)
        kg = jnp.pad(idx_k_g[l], (0, LANES - IDX_DIM)).reshape(1, LANES)
        kb = jnp.pad(idx_k_b[l], (0, LANES - IDX_DIM)).reshape(1, LANES)
        qa, ka, va, qb, kbb, vb, qi, ki, wi = _in_proj(h, w_pad, cos_t, sin_t, kg, kb, seq, tm)
        oa = _swa(sinks[l], qa, ka, va, batch, seq)
        ob = _dsa(qb, qi, wi, ki, kbb, vb, batch, seq, ck)
        wo = w_o[l].astype(BF16)
        h1, h1b = _out_proj(oa, ob, h, wo[:512], wo[512:], ln1_g[l].reshape(1, -1), ln1_b[l].reshape(1, -1), tm)
        wqt = peer_wq[l].T.astype(BF16)
        sk = peer_subkeys[l].reshape(PEER_HEADS * 2, PEER_NKEYS, PEER_DKEY // 2).astype(BF16)
        tabs = _peer_route(h1b, wqt, sk, tb_route)
        u = peer_u[l].astype(BF16)
        vt = peer_v[l].T.astype(BF16)
        h = _peer_ffn(h1b, h1, u, vt, tabs, ln2_g[l].reshape(1, -1), ln2_b[l].reshape(1, -1), tb_ffn, eb)
    return h.reshape(batch, seq, D_MODEL)
```

```python
import functools
import math

import jax
import jax.numpy as jnp
from jax import lax
from jax.experimental import pallas as pl
from jax.experimental.pallas import tpu as pltpu

F32 = jnp.float32
BF16 = jnp.bfloat16
I32 = jnp.int32

D_MODEL = 1024
HEAD_DIM = 64
SWA_HEADS = 8
SWA_KV_HEADS = 2
WINDOW = 128
BLOCK = 128
DSA_HEADS = 8
IDX_HEADS = 4
IDX_DIM = 64
TOPK_MAX = 256
ROPE_THETA = 10000.0
PEER_HEADS = 8
PEER_NKEYS = 128
PEER_DKEY = 256
PEER_TOPK = 16
LN_EPS = 1e-5
DEPTH = 1
ALPHA = (2.0 * DEPTH) ** 0.25

LANES = 128
BF16_ROWS = 16
VMEM_LIMIT_BYTES = 56 * 1024 * 1024

MIX_COLS = 1732
MIX_COLS_PAD = 1792
WI_LANE0 = 64

NEG_BIG = -1e30
INT_MIN = -(2 ** 31)

_NT = (((1,), (1,)), ((), ()))


def _cparams(sem):
    return pltpu.CompilerParams(dimension_semantics=sem, vmem_limit_bytes=VMEM_LIMIT_BYTES)


def _layer_norm_rows(z, g, b):
    mu = jnp.mean(z, axis=-1, keepdims=True)
    zc = z - mu
    var = jnp.mean(zc * zc, axis=-1, keepdims=True)
    return zc * lax.rsqrt(var + LN_EPS) * g + b


def _in_proj_kernel(x_ref, w_ref, cos_ref, sin_ref, kg_ref, kb_ref,
                    qa_ref, ka_ref, va_ref, qb_ref, kb_out_ref, vb_ref, qi_ref, ki_ref, wi_ref):
    xb = x_ref[...].astype(BF16)
    proj = jnp.dot(xb, w_ref[...], preferred_element_type=F32)
    cos = cos_ref[...]
    sin = sin_ref[...]
    lane = lax.broadcasted_iota(I32, cos.shape, 1)
    first_half = (lane % HEAD_DIM) < (HEAD_DIM // 2)

    def rope(t):
        swapped = jnp.where(first_half, pltpu.roll(t, LANES - HEAD_DIM // 2, 1),
                            pltpu.roll(t, HEAD_DIM // 2, 1))
        return t * cos + swapped * sin

    def slab(i):
        return proj[:, i * LANES:(i + 1) * LANES]

    q_scale = HEAD_DIM ** -0.5
    for i in range(4):
        qa_ref[:, i * LANES:(i + 1) * LANES] = (rope(slab(i)) * q_scale).astype(BF16)
        qb_ref[:, i * LANES:(i + 1) * LANES] = (rope(slab(6 + i)) * q_scale).astype(BF16)
    ka_ref[...] = rope(slab(4)).astype(BF16)
    va_ref[...] = slab(5).astype(BF16)
    kv = slab(10)
    kv_r = rope(kv)
    kb_out_ref[...] = kv_r[:, :HEAD_DIM].astype(BF16)
    vb_ref[...] = jnp.where(lane < HEAD_DIM, pltpu.roll(kv, HEAD_DIM, 1),
                            jnp.where(lane == HEAD_DIM, 1.0, 0.0)).astype(BF16)
    for i in range(2):
        qi_ref[:, i * LANES:(i + 1) * LANES] = rope(slab(11 + i)).astype(BF16)
    last = slab(13)
    is_ki = lane < IDX_DIM
    mu = jnp.sum(jnp.where(is_ki, last, 0.0), axis=-1, keepdims=True) * (1.0 / IDX_DIM)
    cen = last - mu
    var = jnp.sum(jnp.where(is_ki, cen * cen, 0.0), axis=-1, keepdims=True) * (1.0 / IDX_DIM)
    kin = cen * lax.rsqrt(var + LN_EPS) * kg_ref[...] + kb_ref[...]
    kin = jnp.where(is_ki, kin, 0.0)
    ki_ref[...] = rope(kin)[:, :IDX_DIM].astype(BF16)
    w_scale = (IDX_HEADS ** -0.5) * (IDX_DIM ** -0.5)
    wi_ref[...] = last * w_scale


def _in_proj(x2, w_pad, cos_t, sin_t, kg, kb, seq, tm):
    n = x2.shape[0]
    nt = n // tm
    tps = seq // tm
    row = lambda w: pl.BlockSpec((tm, w), lambda i: (i, 0))
    out_shapes = [
        jax.ShapeDtypeStruct((n, 512), BF16), jax.ShapeDtypeStruct((n, 128), BF16),
        jax.ShapeDtypeStruct((n, 128), BF16), jax.ShapeDtypeStruct((n, 512), BF16),
        jax.ShapeDtypeStruct((n, 64), BF16), jax.ShapeDtypeStruct((n, 128), BF16),
        jax.ShapeDtypeStruct((n, 256), BF16), jax.ShapeDtypeStruct((n, 64), BF16),
        jax.ShapeDtypeStruct((n, 128), F32),
    ]
    return pl.pallas_call(
        _in_proj_kernel,
        grid=(nt,),
        in_specs=[
            row(D_MODEL),
            pl.BlockSpec((D_MODEL, MIX_COLS_PAD), lambda i: (0, 0)),
            pl.BlockSpec((tm, LANES), lambda i: (i % tps, 0)),
            pl.BlockSpec((tm, LANES), lambda i: (i % tps, 0)),
            pl.BlockSpec((1, LANES), lambda i: (0, 0)),
            pl.BlockSpec((1, LANES), lambda i: (0, 0)),
        ],
        out_specs=[row(512), row(128), row(128), row(512), row(64), row(128), row(256), row(64), row(128)],
        out_shape=out_shapes,
        compiler_params=_cparams(("parallel",)),
        name="in_proj",
    )(x2, w_pad, cos_t, sin_t, kg, kb)


def _swa_kernel(sink_ref, q_ref, kp_ref, kc_ref, vp_ref, vc_ref, o_ref):
    n = pl.program_id(1)
    qi = lax.broadcasted_iota(I32, (BLOCK, 2 * BLOCK), 0)
    sj = lax.broadcasted_iota(I32, (BLOCK, 2 * BLOCK), 1)
    diff = qi + BLOCK - sj
    valid = (diff >= 0) & (diff < WINDOW) & ((n > 0) | (sj >= BLOCK))
    group = SWA_HEADS // SWA_KV_HEADS
    for kh in range(SWA_KV_HEADS):
        cs = slice(kh * HEAD_DIM, (kh + 1) * HEAD_DIM)
        k = jnp.concatenate([kp_ref[:, cs], kc_ref[:, cs]], axis=0)
        v = jnp.concatenate([vp_ref[:, cs], vc_ref[:, cs]], axis=0)
        for g in range(group):
            h = kh * group + g
            hs = slice(h * HEAD_DIM, (h + 1) * HEAD_DIM)
            s = lax.dot_general(q_ref[:, hs], k, _NT, preferred_element_type=F32)
            s = jnp.where(valid, s, NEG_BIG)
            sink = sink_ref[h]
            m = jnp.maximum(jnp.max(s, axis=-1, keepdims=True), sink)
            p = jnp.exp(s - m)
            denom = jnp.sum(p, axis=-1, keepdims=True) + jnp.exp(sink - m)
            o = jnp.dot(p.astype(BF16), v, preferred_element_type=F32)
            o_ref[:, hs] = (o / denom).astype(o_ref.dtype)


def _swa(sinks, qa, ka, va, batch, seq):
    nb = seq // BLOCK
    cur = lambda w: pl.BlockSpec((BLOCK, w), lambda b, n: (b * nb + n, 0))
    prev = lambda w: pl.BlockSpec((BLOCK, w), lambda b, n: (b * nb + jnp.maximum(n - 1, 0), 0))
    return pl.pallas_call(
        _swa_kernel,
        grid=(batch, nb),
        in_specs=[pl.BlockSpec(memory_space=pltpu.SMEM), cur(512), prev(128), cur(128), prev(128), cur(128)],
        out_specs=cur(512),
        out_shape=jax.ShapeDtypeStruct((batch * seq, 512), BF16),
        compiler_params=_cparams(("parallel", "parallel")),
        name="swa",
    )(sinks, qa, ka, ka, va, va)


DSA_ROWS = 2 * BLOCK
HI_BITS = 14


def _dsa_kernel(qb_ref, qi_ref, wi_ref, ki_ref, kb_ref, vb_ref, o_ref,
                keys_ref, hi_ref, vst_ref, tie_ref, bias_ref, m_ref, acc_ref, qs_ref, s_ref, p_ref, alpha_ref,
                *, ck, nc, k_sel, seq_bits):
    n = pl.program_id(1)
    t0 = n * DSA_ROWS
    nck = (t0 + DSA_ROWS + ck - 1) // ck
    lane_tiles = ck // LANES
    qpos = t0 + lax.broadcasted_iota(I32, (DSA_ROWS, ck), 0)
    lane_pos = lax.broadcasted_iota(I32, (DSA_ROWS, ck), 1)
    tile_lane = lax.broadcasted_iota(I32, (BLOCK, LANES), 1)
    ones = jnp.ones((LANES, LANES), BF16)

    qi = qi_ref[...]
    qi4 = jnp.concatenate([qi[:, h * IDX_DIM:(h + 1) * IDX_DIM] for h in range(IDX_HEADS)], axis=0)
    wi = wi_ref[...]
    wcols = [wi[:, WI_LANE0 + h:WI_LANE0 + h + 1] for h in range(IDX_HEADS)]

    def score_chunk(c, causal):
        r = lax.dot_general(qi4, ki_ref[c], _NT, preferred_element_type=F32)
        r = jnp.maximum(r, 0.0)
        isc = wcols[0] * r[0:DSA_ROWS]
        for h in range(1, IDX_HEADS):
            isc = isc + wcols[h] * r[h * DSA_ROWS:(h + 1) * DSA_ROWS]
        isc = jnp.where(isc == 0.0, 0.0, isc)
        bits = pltpu.bitcast(isc, I32)
        key = bits ^ ((bits >> 31) & 0x7FFFFFFF)
        if causal:
            key = jnp.where(c * ck + lane_pos <= qpos, key, INT_MIN)
        keys_ref[c] = key
        hi_ref[c] = pltpu.bitcast(hi_pattern(lax.shift_right_logical(key ^ INT_MIN, 32 - HI_BITS)),
                                  jnp.uint32)

    def hi_pattern(d):
        return pltpu.bitcast(lax.shift_left(d + 128, 16), F32).astype(BF16)

    def scan_chunks(lo, hi, causal):
        def body(c, carry):
            score_chunk(c, causal)
            return carry
        lax.fori_loop(lo, hi, body, 0)

    n_past = (t0 + 1) // ck
    scan_chunks(0, n_past, False)
    scan_chunks(n_past, nck, True)

    def row_total(acc):
        return jnp.dot(acc.astype(BF16), ones, preferred_element_type=F32)

    def select(v):
        def count(grp, pred):
            acc = jnp.zeros((BLOCK, LANES), F32)
            for c in range(v):
                for j in range(lane_tiles):
                    key = keys_ref[c, grp * BLOCK:(grp + 1) * BLOCK, j * LANES:(j + 1) * LANES]
                    acc = acc + jnp.where(pred(key, c * ck + j * LANES + tile_lane), 1.0, 0.0)
            return acc

        def bit_step(i, carry):
            ta, tb, cand_b_prev, acc_b = carry
            bit = lax.shift_left(jnp.int32(1), 31 - i)
            tot_b = row_total(acc_b)
            cand_a = ta | bit
            cand_a_s = cand_a ^ INT_MIN
            acc_a = count(0, lambda key, pos: key >= cand_a_s)
            tb = jnp.where(tot_b >= k_sel, cand_b_prev, tb)
            cand_b = tb | bit
            cand_b_s = cand_b ^ INT_MIN
            acc_b = count(1, lambda key, pos: key >= cand_b_s)
            ta = jnp.where(row_total(acc_a) >= k_sel, cand_a, ta)
            return ta, tb, cand_b, acc_b

        hw = hi_ref.shape[1] // 2

        def count_hi(grp, cand):
            acc = jnp.zeros((BLOCK, LANES), BF16)
            for c in range(v):
                for j in range(lane_tiles):
                    hi = pltpu.bitcast(hi_ref[c, grp * hw:(grp + 1) * hw, j * LANES:(j + 1) * LANES], BF16)
                    acc = acc + jnp.where(hi >= cand, jnp.ones((), BF16), jnp.zeros((), BF16))
            return acc

        def hi_step(i, carry):
            ta, tb, cand_b_prev, acc_b = carry
            bit = lax.shift_left(jnp.int32(1), HI_BITS - 1 - i)
            tot_b = row_total(acc_b)
            cand_a = ta | bit
            acc_a = count_hi(0, hi_pattern(cand_a))
            tb = jnp.where(tot_b >= k_sel, cand_b_prev, tb)
            cand_b = tb | bit
            acc_b = count_hi(1, hi_pattern(cand_b))
            ta = jnp.where(row_total(acc_a) >= k_sel, cand_a, ta)
            return ta, tb, cand_b, acc_b

        zi = jnp.zeros((BLOCK, LANES), I32)
        ta, tb, cand_b, acc_b = lax.fori_loop(0, HI_BITS, hi_step, (zi, zi, zi, jnp.zeros((BLOCK, LANES), BF16)))
        tb = jnp.where(row_total(acc_b) >= k_sel, cand_b, tb)
        ta = lax.shift_left(ta, 32 - HI_BITS)
        tb = lax.shift_left(tb, 32 - HI_BITS)
        ta, tb, cand_b, acc_b = lax.fori_loop(HI_BITS, 32, bit_step,
                                              (ta, tb, zi, jnp.zeros((BLOCK, LANES), F32)))
        tb = jnp.where(row_total(acc_b) >= k_sel, cand_b, tb)
        vstar = (ta ^ INT_MIN, tb ^ INT_MIN)
        big = jnp.full((BLOCK, LANES), 2 ** seq_bits, I32)
        need, surplus = [], []
        for grp in range(2):
            vs = vstar[grp]
            vst_ref[grp * BLOCK:(grp + 1) * BLOCK, :] = vs
            tie_ref[grp * BLOCK:(grp + 1) * BLOCK, :] = big
            cnt_gt = row_total(count(grp, lambda key, pos: key > vs))
            cnt_eq = row_total(count(grp, lambda key, pos: key == vs))
            need.append(k_sel - cnt_gt)
            surplus.append(jnp.max(cnt_eq - need[grp]))

        @pl.when(jnp.maximum(surplus[0], surplus[1]) > 0.0)
        def _():
            for grp in range(2):
                vs = vstar[grp]

                def idx_step(i, p):
                    cand = p | lax.shift_left(jnp.int32(1), seq_bits - 1 - i)
                    cnt = row_total(count(grp, lambda key, pos: (key == vs) & (pos < cand)))
                    return jnp.where(cnt < need[grp], cand, p)

                tie_ref[grp * BLOCK:(grp + 1) * BLOCK, :] = lax.fori_loop(0, seq_bits, idx_step, zi)

    for v in range(1, nc + 1):
        pl.when(nck == v)(functools.partial(select, v))

    qb = qb_ref[...]
    for h in range(DSA_HEADS):
        qs_ref[h * DSA_ROWS:(h + 1) * DSA_ROWS, :] = qb[:, h * HEAD_DIM:(h + 1) * HEAD_DIM]
    m_ref[...] = jnp.full(m_ref.shape, NEG_BIG, F32)
    acc_ref[...] = jnp.zeros(acc_ref.shape, F32)
    row_lane = lax.broadcasted_iota(I32, (DSA_ROWS, LANES), 1)

    def stage_a(it, par):
        ca = jnp.minimum(it, nck - 1)
        s_ref[par] = lax.dot_general(qs_ref[...], kb_ref[ca], _NT, preferred_element_type=F32)

    def stage_b(it, par):
        cb = it - 1
        vst = vst_ref[...]
        tie = tie_ref[...]
        for j in range(lane_tiles):
            key = keys_ref[cb, :, j * LANES:(j + 1) * LANES]
            pos = cb * ck + j * LANES + row_lane
            sel = ((key > vst) | ((key == vst) & (pos <= tie))) & (key > INT_MIN)
            bias_ref[:, j * LANES:(j + 1) * LANES] = jnp.where(sel, 0.0, NEG_BIG)
        sb = 1 - par
        for h in range(DSA_HEADS):
            rows = slice(h * DSA_ROWS, (h + 1) * DSA_ROWS)
            s = s_ref[sb, rows, :] + bias_ref[...]
            s_ref[sb, rows, :] = s
            part = s[:, 0:LANES]
            for j in range(1, lane_tiles):
                part = jnp.maximum(part, s[:, j * LANES:(j + 1) * LANES])
            m_old = m_ref[h]
            m_new = jnp.maximum(m_old, jnp.max(part, axis=-1, keepdims=True))
            alpha_ref[sb, h] = jnp.exp(m_old - m_new)
            m_ref[h] = m_new
        for h in range(DSA_HEADS):
            rows = slice(h * DSA_ROWS, (h + 1) * DSA_ROWS)
            p_ref[sb, rows, :] = jnp.exp(s_ref[sb, rows, :] - m_ref[h]).astype(BF16)

    def stage_c(it, par):
        cc = jnp.maximum(it - 2, 0)
        pv = jnp.dot(p_ref[par], vb_ref[cc], preferred_element_type=F32)
        for h in range(DSA_HEADS):
            acc_ref[h] = alpha_ref[par, h] * acc_ref[h] + pv[h * DSA_ROWS:(h + 1) * DSA_ROWS]

    def attend_step(it, carry):
        for par in range(2):
            @pl.when(it % 2 == par)
            def _():
                stage_a(it, par)
                stage_b(it, par)
                stage_c(it, par)
        return carry

    p_ref[1] = jnp.zeros(p_ref.shape[1:], BF16)
    alpha_ref[1] = jnp.ones(alpha_ref.shape[1:], F32)
    stage_a(0, 0)
    lax.fori_loop(1, nck + 1, attend_step, 0)
    for par in range(2):
        pl.when((nck + 1) % 2 == par)(functools.partial(stage_c, nck + 1, par))
    for h in range(DSA_HEADS):
        a = acc_ref[h]
        o_ref[:, h * HEAD_DIM:(h + 1) * HEAD_DIM] = (a[:, :HEAD_DIM] / a[:, HEAD_DIM:HEAD_DIM + 1]).astype(o_ref.dtype)


def _dsa(qb, qi, wi, ki, kb, vb, batch, seq, ck):
    nq = seq // DSA_ROWS
    nc = seq // ck
    k_sel = min(TOPK_MAX, seq // 4)
    seq_bits = int(math.log2(seq))
    assert 2 ** seq_bits == seq and k_sel <= ck and seq % DSA_ROWS == 0
    assert seq // LANES <= 256
    chunked = lambda a: a.reshape(batch, nc, ck, a.shape[-1])
    blk = lambda w: pl.BlockSpec((DSA_ROWS, w), lambda b, n: (b * nq + n, 0))
    whole = lambda w: pl.BlockSpec((None, nc, ck, w), lambda b, n: (b, 0, 0, 0))
    kernel = functools.partial(_dsa_kernel, ck=ck, nc=nc, k_sel=k_sel, seq_bits=seq_bits)
    return pl.pallas_call(
        kernel,
        grid=(batch, nq),
        in_specs=[blk(512), blk(256), blk(128), whole(IDX_DIM), whole(HEAD_DIM), whole(LANES)],
        out_specs=blk(512),
        out_shape=jax.ShapeDtypeStruct((batch * seq, 512), BF16),
        scratch_shapes=[
            pltpu.VMEM((nc, DSA_ROWS, ck), I32),
            pltpu.VMEM((nc, DSA_ROWS * jnp.dtype(BF16).itemsize // 4, ck), jnp.uint32),
            pltpu.VMEM((DSA_ROWS, LANES), I32),
            pltpu.VMEM((DSA_ROWS, LANES), I32),
            pltpu.VMEM((DSA_ROWS, ck), F32),
            pltpu.VMEM((DSA_HEADS, DSA_ROWS, 1), F32),
            pltpu.VMEM((DSA_HEADS, DSA_ROWS, LANES), F32),
            pltpu.VMEM((DSA_HEADS * DSA_ROWS, HEAD_DIM), BF16),
            pltpu.VMEM((2, DSA_HEADS * DSA_ROWS, ck), F32),
            pltpu.VMEM((2, DSA_HEADS * DSA_ROWS, ck), BF16),
            pltpu.VMEM((2, DSA_HEADS, DSA_ROWS, 1), F32),
        ],
        compiler_params=_cparams(("parallel", "arbitrary")),
        name="dsa",
    )(qb, qi, wi, chunked(ki), chunked(kb), chunked(vb))


def _out_proj_kernel(oa_ref, ob_ref, x_ref, wa_ref, wb_ref, g_ref, b_ref, h_ref, hb_ref):
    mix = jnp.dot(oa_ref[...], wa_ref[...], preferred_element_type=F32)
    mix = mix + jnp.dot(ob_ref[...], wb_ref[...], preferred_element_type=F32)
    h = _layer_norm_rows(ALPHA * x_ref[...] + mix, g_ref[...], b_ref[...])
    h_ref[...] = h
    hb_ref[...] = h.astype(BF16)


def _out_proj(oa, ob, x2, wo_a, wo_b, g, b, tm):
    n = x2.shape[0]
    row = lambda w: pl.BlockSpec((tm, w), lambda i: (i, 0))
    const = lambda r, c: pl.BlockSpec((r, c), lambda i: (0, 0))
    return pl.pallas_call(
        _out_proj_kernel,
        grid=(n // tm,),
        in_specs=[row(512), row(512), row(D_MODEL), const(512, D_MODEL), const(512, D_MODEL),
                  const(1, D_MODEL), const(1, D_MODEL)],
        out_specs=[row(D_MODEL), row(D_MODEL)],
        out_shape=[jax.ShapeDtypeStruct((n, D_MODEL), F32), jax.ShapeDtypeStruct((n, D_MODEL), BF16)],
        compiler_params=_cparams(("parallel",)),
        name="out_proj",
    )(oa, ob, x2, wo_a, wo_b, g, b)


def _top_values(s, count, with_rank):
    vals = []
    rank = jnp.full(s.shape, float(count), F32) if with_rank else None
    cur = s
    for j in range(count):
        m = jnp.max(cur, axis=0, keepdims=True)
        vals.append(m)
        below = cur < m
        cur = jnp.where(below, cur, -jnp.inf)
        if with_rank:
            rank = jnp.where(below, rank, jnp.minimum(rank, float(j)))
    return vals, rank


SUBLANES = 8


def _oddeven_merge_sort_pairs(n):
    pairs = []

    def merge(lo, hi, r):
        step = r * 2
        if step < hi - lo:
            merge(lo, hi, step)
            merge(lo + r, hi, step)
            pairs.extend((i, i + r) for i in range(lo + r, hi - r, step))
        else:
            pairs.append((lo, lo + r))

    def sort(lo, hi):
        if hi - lo >= 1:
            mid = lo + (hi - lo) // 2
            sort(lo, mid)
            sort(mid + 1, hi)
            merge(lo, hi, 1)

    sort(0, n - 1)
    return pairs


def _sorted_top16(s):
    n = PEER_NKEYS // SUBLANES
    x = [s[i * SUBLANES:(i + 1) * SUBLANES, :] for i in range(n)]
    for i, j in _oddeven_merge_sort_pairs(n):
        x[i], x[j] = jnp.maximum(x[i], x[j]), jnp.minimum(x[i], x[j])
    for shift in (4, 2, 1):
        y = [jnp.maximum(x[i], pltpu.roll(x[n - 1 - i], shift, 0)) for i in range(n)]
        for stride in (8, 4, 2, 1):
            for i in range(n):
                if i & stride == 0:
                    y[i], y[i + stride] = jnp.maximum(y[i], y[i + stride]), jnp.minimum(y[i], y[i + stride])
        x = y
    return x


def _peer_route_kernel(h_ref, wqt_ref, sk_ref, r2_ref, p2_ref, na_ref, p1_ref):
    qt = lax.dot_general(wqt_ref[...], h_ref[...], _NT, preferred_element_type=F32)
    half = PEER_DKEY // 2
    for h in range(PEER_HEADS):
        s = []
        for p in range(2):
            idx = h * 2 + p
            qhp = qt[idx * half:(idx + 1) * half, :].astype(BF16)
            s.append(jnp.dot(sk_ref[idx], qhp, preferred_element_type=F32))
        s1, s2 = s
        v1 = [x[0:1] for x in _sorted_top16(s1)]
        top2 = _sorted_top16(s2)
        v2 = [x[0:1] for x in top2]
        rank2 = []
        for i in range(PEER_NKEYS // SUBLANES):
            blk = s2[i * SUBLANES:(i + 1) * SUBLANES, :]
            r = jnp.where(blk < top2[0], 1.0, 0.0)
            for j in range(1, PEER_TOPK):
                r = r + jnp.where(blk < top2[j], 1.0, 0.0)
            rank2.append(r)
        rank2 = jnp.concatenate(rank2, axis=0)
        v2_all = jnp.concatenate(v2, axis=0)
        sub = lax.broadcasted_iota(I32, (8, v2_all.shape[1]), 0)
        cands = [v1[0] + v2_all]
        for i in range(1, 8):
            lim = PEER_TOPK // (i + 1)
            cands.append(jnp.where(sub < lim, v1[i] + v2_all[:8], -jnp.inf))
        cands.append(jnp.concatenate(v1[8:], axis=0) + v2[0])
        cand = jnp.concatenate(cands, axis=0)
        tops, _ = _top_values(cand, PEER_TOPK, False)
        tau = tops[-1]
        tau = jnp.where(tau > -jnp.inf, tau, jnp.min(jnp.where(cand > -jnp.inf, cand, jnp.inf), axis=0,
                                                     keepdims=True))
        mx = v1[0] + v2[0]
        z = jnp.sum(jnp.where(cand >= tau, jnp.exp(cand - mx), 0.0), axis=0, keepdims=True)
        na = jnp.zeros(s1.shape, F32)
        for j in range(8):
            na = na + jnp.where((s1 + v2[j]) >= tau, 1.0, 0.0)
        n_top = jnp.sum(jnp.where((v1[0] + v2_all[8:]) >= tau, 1.0, 0.0), axis=0, keepdims=True)
        na = na + jnp.where(s1 == v1[0], n_top, 0.0)
        r2_ref[h] = pltpu.bitcast(rank2.astype(BF16), jnp.uint32)
        p2_ref[h] = pltpu.bitcast(jnp.exp(s2 - v2[0]).astype(BF16), jnp.uint32)
        na_ref[h] = na
        p1_ref[h] = jnp.exp(s1 - v1[0]) / z


def _peer_route(hb, wqt, sk, tb):
    n = hb.shape[0]
    word_rows = PEER_NKEYS * jnp.dtype(BF16).itemsize // 4
    tab16 = jax.ShapeDtypeStruct((PEER_HEADS, word_rows, n), jnp.uint32)
    tab32 = jax.ShapeDtypeStruct((PEER_HEADS, PEER_NKEYS, n), F32)
    spec16 = pl.BlockSpec((PEER_HEADS, word_rows, tb), lambda i: (0, 0, i))
    spec32 = pl.BlockSpec((PEER_HEADS, PEER_NKEYS, tb), lambda i: (0, 0, i))
    return pl.pallas_call(
        _peer_route_kernel,
        grid=(n // tb,),
        in_specs=[pl.BlockSpec((tb, D_MODEL), lambda i: (i, 0)),
                  pl.BlockSpec(wqt.shape, lambda i: (0, 0)),
                  pl.BlockSpec(sk.shape, lambda i: (0, 0, 0))],
        out_specs=[spec16, spec16, spec32, spec32],
        out_shape=[tab16, tab16, tab32, tab32],
        compiler_params=_cparams(("parallel",)),
        name="peer_route",
    )(hb, wqt, sk)


def _peer_ffn_kernel(hb_ref, h_ref, u_ref, vt_ref, r2_ref, p2_ref, na_ref, p1_ref, g_ref, b_ref,
                     o_ref, acc_ref, w_ref, *, tb, eb, sub):
    j = pl.program_id(1)

    @pl.when(j == 0)
    def _():
        acc_ref[...] = jnp.zeros(acc_ref.shape, F32)

    groups = PEER_NKEYS // BF16_ROWS
    wr = r2_ref.shape[1] // groups
    zero = jnp.zeros((), BF16)
    for sb in range(eb // sub):
        srows = slice(sb * sub, (sb + 1) * sub)
        act_t = lax.dot_general(u_ref[srows, :], hb_ref[...], _NT, preferred_element_type=F32)
        for asub in range(sub // PEER_NKEYS):
            al = sb * (sub // PEER_NKEYS) + asub
            for tt in range(tb // LANES):
                cols = slice(tt * LANES, (tt + 1) * LANES)
                gate = [jnp.zeros((BF16_ROWS, LANES), BF16) for _ in range(groups)]
                for h in range(PEER_HEADS):
                    na = jnp.broadcast_to(na_ref[h, al:al + 1, cols], (BF16_ROWS, LANES)).astype(BF16)
                    p1 = jnp.broadcast_to(p1_ref[h, al:al + 1, cols], (BF16_ROWS, LANES)).astype(BF16)
                    for g in range(groups):
                        ws = slice(g * wr, (g + 1) * wr)
                        r2 = pltpu.bitcast(r2_ref[h, ws, cols], BF16)
                        p2 = pltpu.bitcast(p2_ref[h, ws, cols], BF16)
                        gate[g] = gate[g] + jnp.where(r2 < na, p2, zero) * p1
                for g in range(groups):
                    lrows = slice(asub * PEER_NKEYS + g * BF16_ROWS, asub * PEER_NKEYS + (g + 1) * BF16_ROWS)
                    x = act_t[lrows, cols]
                    gelu = 0.5 * x * (1.0 + lax.erf(x * (2.0 ** -0.5)))
                    w_ref[sb, lrows, cols] = gelu.astype(BF16) * gate[g]
    w_all = w_ref[...].reshape(eb, tb)
    acc_ref[...] += jnp.dot(vt_ref[...], w_all, preferred_element_type=F32)

    @pl.when(j == pl.num_programs(1) - 1)
    def _():
        z = ALPHA * h_ref[...] + acc_ref[...].T
        o_ref[...] = _layer_norm_rows(z, g_ref[...], b_ref[...])


def _peer_ffn(hb, h, u, vt, tabs, g, b, tb, eb):
    n = hb.shape[0]
    ne = u.shape[0]
    tab_spec = pl.BlockSpec((PEER_HEADS, tabs[0].shape[1], tb), lambda i, j: (0, 0, i))
    a_spec = pl.BlockSpec((PEER_HEADS, eb // PEER_NKEYS, tb), lambda i, j: (0, j, i))
    sub = 512
    kernel = functools.partial(_peer_ffn_kernel, tb=tb, eb=eb, sub=sub)
    return pl.pallas_call(
        kernel,
        grid=(n // tb, ne // eb),
        in_specs=[pl.BlockSpec((tb, D_MODEL), lambda i, j: (i, 0)),
                  pl.BlockSpec((tb, D_MODEL), lambda i, j: (i, 0)),
                  pl.BlockSpec((eb, D_MODEL), lambda i, j: (j, 0)),
                  pl.BlockSpec((D_MODEL, eb), lambda i, j: (0, j)),
                  tab_spec, tab_spec, a_spec, a_spec,
                  pl.BlockSpec((1, D_MODEL), lambda i, j: (0, 0)),
                  pl.BlockSpec((1, D_MODEL), lambda i, j: (0, 0))],
        out_specs=pl.BlockSpec((tb, D_MODEL), lambda i, j: (i, 0)),
        out_shape=jax.ShapeDtypeStruct((n, D_MODEL), F32),
        scratch_shapes=[pltpu.VMEM((D_MODEL, tb), F32), pltpu.VMEM((eb // sub, sub, tb), BF16)],
        compiler_params=_cparams(("parallel", "arbitrary")),
        name="peer_ffn",
    )(hb, h, u, vt, *tabs, g, b)


def _tiles(n_tokens, seq):
    tm = 256 if seq % 256 == 0 else BLOCK
    ck = 512 if seq >= 2048 else BLOCK
    tb_route = 256 if n_tokens % 256 == 0 else BLOCK
    tb_ffn = 512 if n_tokens % 512 == 0 else BLOCK
    eb = 2048
    return tm, ck, tb_route, tb_ffn, eb


def kernel(x, w_in, idx_k_g, idx_k_b, sinks, w_o, ln1_g, ln1_b, peer_wq, peer_subkeys, peer_u, peer_v,
           ln2_g, ln2_b):
    batch, seq, _ = x.shape
    n = batch * seq
    tm, ck, tb_route, tb_ffn, eb = _tiles(n, seq)

    inv = ROPE_THETA ** (-jnp.arange(0, HEAD_DIM, 2, dtype=F32) / HEAD_DIM)
    ang = jnp.arange(seq, dtype=F32)[:, None] * inv[None, :]
    cos, sin = jnp.cos(ang), jnp.sin(ang)
    cos_t = jnp.concatenate([cos, cos, cos, cos], axis=-1)
    sin_t = jnp.concatenate([-sin, sin, -sin, sin], axis=-1)

    h = x.reshape(n, D_MODEL)
    for l in range(DEPTH):
        w_pad = jnp.pad(w_in[l], ((0, 0), (0, MIX_COLS_PAD - MIX_COLS))).astype(BF16)
        kg = jnp.pad(idx_k_g[l], (0, LANES - IDX_DIM)).reshape(1, LANES)
        kb = jnp.pad(idx_k_b[l], (0, LANES - IDX_DIM)).reshape(1, LANES)
        qa, ka, va, qb, kbb, vb, qi, ki, wi = _in_proj(h, w_pad, cos_t, sin_t, kg, kb, seq, tm)
        oa = _swa(sinks[l], qa, ka, va, batch, seq)
        ob = _dsa(qb, qi, wi, ki, kbb, vb, batch, seq, ck)
        wo = w_o[l].astype(BF16)
        h1, h1b = _out_proj(oa, ob, h, wo[:512], wo[512:], ln1_g[l].reshape(1, -1), ln1_b[l].reshape(1, -1), tm)
        wqt = peer_wq[l].T.astype(BF16)
        sk = peer_subkeys[l].reshape(PEER_HEADS * 2, PEER_NKEYS, PEER_DKEY // 2).astype(BF16)
        tabs = _peer_route(h1b, wqt, sk, tb_route)
        u = peer_u[l].astype(BF16)
        vt = peer_v[l].T.astype(BF16)
        h = _peer_ffn(h1b, h1, u, vt, tabs, ln2_g[l].reshape(1, -1), ln2_b[l].reshape(1, -1), tb_ffn, eb)
    return h.reshape(batch, seq, D_MODEL)
```

```python
import functools
import math

import jax
import jax.numpy as jnp
from jax import lax
from jax.experimental import pallas as pl
from jax.experimental.pallas import tpu as pltpu

F32 = jnp.float32
BF16 = jnp.bfloat16
I32 = jnp.int32

D_MODEL = 1024
HEAD_DIM = 64
SWA_HEADS = 8
SWA_KV_HEADS = 2
WINDOW = 128
BLOCK = 128
DSA_HEADS = 8
IDX_HEADS = 4
IDX_DIM = 64
TOPK_MAX = 256
ROPE_THETA = 10000.0
PEER_HEADS = 8
PEER_NKEYS = 128
PEER_DKEY = 256
PEER_TOPK = 16
LN_EPS = 1e-5
DEPTH = 1
ALPHA = (2.0 * DEPTH) ** 0.25

LANES = 128
SUBLANES = 8
BF16_ROWS = 16
VMEM_LIMIT_BYTES = 56 * 1024 * 1024

MIX_COLS = 1732
MIX_COLS_PAD = 1792
WI_LANE0 = 64

NEG_BIG = -1e30
INT_MIN = -(2 ** 31)

_NT = (((1,), (1,)), ((), ()))


def _cparams(sem):
    return pltpu.CompilerParams(dimension_semantics=sem, vmem_limit_bytes=VMEM_LIMIT_BYTES)


def _layer_norm_rows(z, g, b):
    mu = jnp.mean(z, axis=-1, keepdims=True)
    zc = z - mu
    var = jnp.mean(zc * zc, axis=-1, keepdims=True)
    return zc * lax.rsqrt(var + LN_EPS) * g + b


def _in_proj_kernel(x_ref, w_ref, cos_ref, sin_ref, kg_ref, kb_ref,
                    qa_ref, ka_ref, va_ref, qb_ref, kb_out_ref, vb_ref, qi_ref, ki_ref, wi_ref):
    xb = x_ref[...].astype(BF16)
    proj = jnp.dot(xb, w_ref[...], preferred_element_type=F32)
    cos = cos_ref[...]
    sin = sin_ref[...]
    lane = lax.broadcasted_iota(I32, cos.shape, 1)
    first_half = (lane % HEAD_DIM) < (HEAD_DIM // 2)

    def rope(t):
        swapped = jnp.where(first_half, pltpu.roll(t, LANES - HEAD_DIM // 2, 1),
                            pltpu.roll(t, HEAD_DIM // 2, 1))
        return t * cos + swapped * sin

    def slab(i):
        return proj[:, i * LANES:(i + 1) * LANES]

    q_scale = HEAD_DIM ** -0.5
    for i in range(4):
        qa_ref[:, i * LANES:(i + 1) * LANES] = (rope(slab(i)) * q_scale).astype(BF16)
        qb_ref[:, i * LANES:(i + 1) * LANES] = (rope(slab(6 + i)) * q_scale).astype(BF16)
    ka_ref[...] = rope(slab(4)).astype(BF16)
    va_ref[...] = slab(5).astype(BF16)
    kv = slab(10)
    kv_r = rope(kv)
    kb_out_ref[...] = kv_r[:, :HEAD_DIM].astype(BF16)
    vb_ref[...] = jnp.where(lane < HEAD_DIM, pltpu.roll(kv, HEAD_DIM, 1),
                            jnp.where(lane == HEAD_DIM, 1.0, 0.0)).astype(BF16)
    for i in range(2):
        qi_ref[:, i * LANES:(i + 1) * LANES] = rope(slab(11 + i)).astype(BF16)
    last = slab(13)
    is_ki = lane < IDX_DIM
    mu = jnp.sum(jnp.where(is_ki, last, 0.0), axis=-1, keepdims=True) * (1.0 / IDX_DIM)
    cen = last - mu
    var = jnp.sum(jnp.where(is_ki, cen * cen, 0.0), axis=-1, keepdims=True) * (1.0 / IDX_DIM)
    kin = cen * lax.rsqrt(var + LN_EPS) * kg_ref[...] + kb_ref[...]
    kin = jnp.where(is_ki, kin, 0.0)
    ki_ref[...] = rope(kin)[:, :IDX_DIM].astype(BF16)
    w_scale = (IDX_HEADS ** -0.5) * (IDX_DIM ** -0.5)
    wi_ref[...] = last * w_scale


def _in_proj(x2, w_pad, cos_t, sin_t, kg, kb, seq, tm):
    n = x2.shape[0]
    nt = n // tm
    tps = seq // tm
    row = lambda w: pl.BlockSpec((tm, w), lambda i: (i, 0))
    out_shapes = [
        jax.ShapeDtypeStruct((n, 512), BF16), jax.ShapeDtypeStruct((n, 128), BF16),
        jax.ShapeDtypeStruct((n, 128), BF16), jax.ShapeDtypeStruct((n, 512), BF16),
        jax.ShapeDtypeStruct((n, 64), BF16), jax.ShapeDtypeStruct((n, 128), BF16),
        jax.ShapeDtypeStruct((n, 256), BF16), jax.ShapeDtypeStruct((n, 64), BF16),
        jax.ShapeDtypeStruct((n, 128), F32),
    ]
    return pl.pallas_call(
        _in_proj_kernel,
        grid=(nt,),
        in_specs=[
            row(D_MODEL),
            pl.BlockSpec((D_MODEL, MIX_COLS_PAD), lambda i: (0, 0)),
            pl.BlockSpec((tm, LANES), lambda i: (i % tps, 0)),
            pl.BlockSpec((tm, LANES), lambda i: (i % tps, 0)),
            pl.BlockSpec((1, LANES), lambda i: (0, 0)),
            pl.BlockSpec((1, LANES), lambda i: (0, 0)),
        ],
        out_specs=[row(512), row(128), row(128), row(512), row(64), row(128), row(256), row(64), row(128)],
        out_shape=out_shapes,
        compiler_params=_cparams(("parallel",)),
        name="in_proj",
    )(x2, w_pad, cos_t, sin_t, kg, kb)


def _swa_kernel(sink_ref, q_ref, kp_ref, kc_ref, vp_ref, vc_ref, o_ref):
    n = pl.program_id(1)
    qi = lax.broadcasted_iota(I32, (BLOCK, 2 * BLOCK), 0)
    sj = lax.broadcasted_iota(I32, (BLOCK, 2 * BLOCK), 1)
    diff = qi + BLOCK - sj
    valid = (diff >= 0) & (diff < WINDOW) & ((n > 0) | (sj >= BLOCK))
    group = SWA_HEADS // SWA_KV_HEADS
    for kh in range(SWA_KV_HEADS):
        cs = slice(kh * HEAD_DIM, (kh + 1) * HEAD_DIM)
        k = jnp.concatenate([kp_ref[:, cs], kc_ref[:, cs]], axis=0)
        v = jnp.concatenate([vp_ref[:, cs], vc_ref[:, cs]], axis=0)
        for g in range(group):
            h = kh * group + g
            hs = slice(h * HEAD_DIM, (h + 1) * HEAD_DIM)
            s = lax.dot_general(q_ref[:, hs], k, _NT, preferred_element_type=F32)
            s = jnp.where(valid, s, NEG_BIG)
            sink = sink_ref[h]
            m = jnp.maximum(jnp.max(s, axis=-1, keepdims=True), sink)
            p = jnp.exp(s - m)
            denom = jnp.sum(p, axis=-1, keepdims=True) + jnp.exp(sink - m)
            o = jnp.dot(p.astype(BF16), v, preferred_element_type=F32)
            o_ref[:, hs] = (o / denom).astype(o_ref.dtype)


def _swa(sinks, qa, ka, va, batch, seq):
    nb = seq // BLOCK
    cur = lambda w: pl.BlockSpec((BLOCK, w), lambda b, n: (b * nb + n, 0))
    prev = lambda w: pl.BlockSpec((BLOCK, w), lambda b, n: (b * nb + jnp.maximum(n - 1, 0), 0))
    return pl.pallas_call(
        _swa_kernel,
        grid=(batch, nb),
        in_specs=[pl.BlockSpec(memory_space=pltpu.SMEM), cur(512), prev(128), cur(128), prev(128), cur(128)],
        out_specs=cur(512),
        out_shape=jax.ShapeDtypeStruct((batch * seq, 512), BF16),
        compiler_params=_cparams(("parallel", "parallel")),
        name="swa",
    )(sinks, qa, ka, ka, va, va)


DSA_ROWS = 2 * BLOCK


def _dsa_kernel(qb_ref, qi_ref, wi_ref, ki_ref, kb_ref, vb_ref, o_ref,
                keys_ref, keyt_ref, vst_ref, tie_ref, bias_ref, m_ref, acc_ref, qs_ref, s_ref, p_ref, alpha_ref,
                *, ck, nc, k_sel, seq_bits):
    n = pl.program_id(1)
    t0 = n * DSA_ROWS
    nck = (t0 + DSA_ROWS + ck - 1) // ck
    lane_tiles = ck // LANES
    qpos = t0 + lax.broadcasted_iota(I32, (DSA_ROWS, ck), 0)
    lane_pos = lax.broadcasted_iota(I32, (DSA_ROWS, ck), 1)

    qi = qi_ref[...]
    qi4 = jnp.concatenate([qi[:, h * IDX_DIM:(h + 1) * IDX_DIM] for h in range(IDX_HEADS)], axis=0)
    wi = wi_ref[...]
    wcols = [wi[:, WI_LANE0 + h:WI_LANE0 + h + 1] for h in range(IDX_HEADS)]

    def score_chunk(c, causal):
        r = lax.dot_general(qi4, ki_ref[c], _NT, preferred_element_type=F32)
        r = jnp.maximum(r, 0.0)
        isc = wcols[0] * r[0:DSA_ROWS]
        for h in range(1, IDX_HEADS):
            isc = isc + wcols[h] * r[h * DSA_ROWS:(h + 1) * DSA_ROWS]
        isc = jnp.where(isc == 0.0, 0.0, isc)
        bits = pltpu.bitcast(isc, I32)
        key = bits ^ ((bits >> 31) & 0x7FFFFFFF)
        if causal:
            key = jnp.where(c * ck + lane_pos <= qpos, key, INT_MIN)
        keys_ref[c] = key
        keyt_ref[c] = pltpu.bitcast(pltpu.bitcast(key, F32).T, I32)

    def scan_chunks(lo, hi, causal):
        def body(c, carry):
            score_chunk(c, causal)
            return carry
        lax.fori_loop(lo, hi, body, 0)

    n_past = (t0 + 1) // ck
    scan_chunks(0, n_past, False)
    scan_chunks(n_past, nck, True)

    sub_pos = lax.broadcasted_iota(I32, (SUBLANES, LANES), 0)
    groups = DSA_ROWS // LANES

    def to_rows(x):
        wide = jnp.broadcast_to(pltpu.bitcast(x[0:1, :], F32), (LANES, LANES))
        return pltpu.bitcast(wide.T, I32)

    def select(v):
        def count(g, pred):
            parts = [jnp.zeros((SUBLANES, LANES), I32) for _ in range(4)]
            i = 0
            for c in range(v):
                for kb in range(ck // SUBLANES):
                    tile = keyt_ref[c, kb * SUBLANES:(kb + 1) * SUBLANES, g * LANES:(g + 1) * LANES]
                    hit = pred(tile, c * ck + kb * SUBLANES + sub_pos)
                    parts[i % 4] = parts[i % 4] + jnp.where(hit, 1, 0)
                    i += 1
            tot = jnp.sum((parts[0] + parts[1]) + (parts[2] + parts[3]), axis=0, keepdims=True)
            return jnp.broadcast_to(tot, (SUBLANES, LANES))

        def bit_step(i, ts):
            bit = lax.shift_left(jnp.int32(1), 31 - i)
            out = []
            for g in range(groups):
                cand = ts[g] | bit
                cand_s = cand ^ INT_MIN
                out.append(jnp.where(count(g, lambda key, pos: key >= cand_s) >= k_sel, cand, ts[g]))
            return tuple(out)

        zi = jnp.zeros((SUBLANES, LANES), I32)
        ts = lax.fori_loop(0, 32, bit_step, (zi,) * groups)
        vstar = [t ^ INT_MIN for t in ts]
        need, surplus = [], []
        for g in range(groups):
            vs = vstar[g]
            vst_ref[g * LANES:(g + 1) * LANES, :] = to_rows(vs)
            tie_ref[g * LANES:(g + 1) * LANES, :] = jnp.full((LANES, LANES), 2 ** seq_bits, I32)
            cnt_gt = count(g, lambda key, pos: key > vs)
            cnt_eq = count(g, lambda key, pos: key == vs)
            need.append(k_sel - cnt_gt)
            surplus.append(jnp.max(cnt_eq - need[g]))

        @pl.when(functools.reduce(jnp.maximum, surplus) > 0)
        def _():
            for g in range(groups):
                vs = vstar[g]

                def idx_step(i, p):
                    cand = p | lax.shift_left(jnp.int32(1), seq_bits - 1 - i)
                    cnt = count(g, lambda key, pos: (key == vs) & (pos < cand))
                    return jnp.where(cnt < need[g], cand, p)

                tie_ref[g * LANES:(g + 1) * LANES, :] = to_rows(lax.fori_loop(0, seq_bits, idx_step, zi))

    for v in range(1, nc + 1):
        pl.when(nck == v)(functools.partial(select, v))

    qb = qb_ref[...]
    for h in range(DSA_HEADS):
        qs_ref[h * DSA_ROWS:(h + 1) * DSA_ROWS, :] = qb[:, h * HEAD_DIM:(h + 1) * HEAD_DIM]
    m_ref[...] = jnp.full(m_ref.shape, NEG_BIG, F32)
    acc_ref[...] = jnp.zeros(acc_ref.shape, F32)
    row_lane = lax.broadcasted_iota(I32, (DSA_ROWS, LANES), 1)

    def stage_a(it, par):
        ca = jnp.minimum(it, nck - 1)
        s_ref[par] = lax.dot_general(qs_ref[...], kb_ref[ca], _NT, preferred_element_type=F32)

    def stage_b(it, par):
        cb = it - 1
        vst = vst_ref[...]
        tie = tie_ref[...]
        for j in range(lane_tiles):
            key = keys_ref[cb, :, j * LANES:(j + 1) * LANES]
            pos = cb * ck + j * LANES + row_lane
            sel = ((key > vst) | ((key == vst) & (pos <= tie))) & (key > INT_MIN)
            bias_ref[:, j * LANES:(j + 1) * LANES] = jnp.where(sel, 0.0, NEG_BIG)
        sb = 1 - par
        for h in range(DSA_HEADS):
            rows = slice(h * DSA_ROWS, (h + 1) * DSA_ROWS)
            s = s_ref[sb, rows, :] + bias_ref[...]
            s_ref[sb, rows, :] = s
            part = s[:, 0:LANES]
            for j in range(1, lane_tiles):
                part = jnp.maximum(part, s[:, j * LANES:(j + 1) * LANES])
            m_old = m_ref[h]
            m_new = jnp.maximum(m_old, jnp.max(part, axis=-1, keepdims=True))
            alpha_ref[sb, h] = jnp.exp(m_old - m_new)
            m_ref[h] = m_new
        for h in range(DSA_HEADS):
            rows = slice(h * DSA_ROWS, (h + 1) * DSA_ROWS)
            p_ref[sb, rows, :] = jnp.exp(s_ref[sb, rows, :] - m_ref[h]).astype(BF16)

    def stage_c(it, par):
        cc = jnp.maximum(it - 2, 0)
        pv = jnp.dot(p_ref[par], vb_ref[cc], preferred_element_type=F32)
        for h in range(DSA_HEADS):
            acc_ref[h] = alpha_ref[par, h] * acc_ref[h] + pv[h * DSA_ROWS:(h + 1) * DSA_ROWS]

    def attend_step(it, carry):
        for par in range(2):
            @pl.when(it % 2 == par)
            def _():
                stage_a(it, par)
                stage_b(it, par)
                stage_c(it, par)
        return carry

    p_ref[1] = jnp.zeros(p_ref.shape[1:], BF16)
    alpha_ref[1] = jnp.ones(alpha_ref.shape[1:], F32)
    stage_a(0, 0)
    lax.fori_loop(1, nck + 1, attend_step, 0)
    for par in range(2):
        pl.when((nck + 1) % 2 == par)(functools.partial(stage_c, nck + 1, par))
    for h in range(DSA_HEADS):
        a = acc_ref[h]
        o_ref[:, h * HEAD_DIM:(h + 1) * HEAD_DIM] = (a[:, :HEAD_DIM] / a[:, HEAD_DIM:HEAD_DIM + 1]).astype(o_ref.dtype)


def _dsa(qb, qi, wi, ki, kb, vb, batch, seq, ck):
    nq = seq // DSA_ROWS
    nc = seq // ck
    k_sel = min(TOPK_MAX, seq // 4)
    seq_bits = int(math.log2(seq))
    assert 2 ** seq_bits == seq and k_sel <= ck and seq % DSA_ROWS == 0
    assert seq // LANES <= 256
    chunked = lambda a: a.reshape(batch, nc, ck, a.shape[-1])
    blk = lambda w: pl.BlockSpec((DSA_ROWS, w), lambda b, n: (b * nq + n, 0))
    whole = lambda w: pl.BlockSpec((None, nc, ck, w), lambda b, n: (b, 0, 0, 0))
    kernel = functools.partial(_dsa_kernel, ck=ck, nc=nc, k_sel=k_sel, seq_bits=seq_bits)
    return pl.pallas_call(
        kernel,
        grid=(batch, nq),
        in_specs=[blk(512), blk(256), blk(128), whole(IDX_DIM), whole(HEAD_DIM), whole(LANES)],
        out_specs=blk(512),
        out_shape=jax.ShapeDtypeStruct((batch * seq, 512), BF16),
        scratch_shapes=[
            pltpu.VMEM((nc, DSA_ROWS, ck), I32),
            pltpu.VMEM((nc, ck, DSA_ROWS), I32),
            pltpu.VMEM((DSA_ROWS, LANES), I32),
            pltpu.VMEM((DSA_ROWS, LANES), I32),
            pltpu.VMEM((DSA_ROWS, ck), F32),
            pltpu.VMEM((DSA_HEADS, DSA_ROWS, 1), F32),
            pltpu.VMEM((DSA_HEADS, DSA_ROWS, LANES), F32),
            pltpu.VMEM((DSA_HEADS * DSA_ROWS, HEAD_DIM), BF16),
            pltpu.VMEM((2, DSA_HEADS * DSA_ROWS, ck), F32),
            pltpu.VMEM((2, DSA_HEADS * DSA_ROWS, ck), BF16),
            pltpu.VMEM((2, DSA_HEADS, DSA_ROWS, 1), F32),
        ],
        compiler_params=_cparams(("parallel", "arbitrary")),
        name="dsa",
    )(qb, qi, wi, chunked(ki), chunked(kb), chunked(vb))


def _out_proj_kernel(oa_ref, ob_ref, x_ref, wa_ref, wb_ref, g_ref, b_ref, h_ref, hb_ref):
    mix = jnp.dot(oa_ref[...], wa_ref[...], preferred_element_type=F32)
    mix = mix + jnp.dot(ob_ref[...], wb_ref[...], preferred_element_type=F32)
    h = _layer_norm_rows(ALPHA * x_ref[...] + mix, g_ref[...], b_ref[...])
    h_ref[...] = h
    hb_ref[...] = h.astype(BF16)


def _out_proj(oa, ob, x2, wo_a, wo_b, g, b, tm):
    n = x2.shape[0]
    row = lambda w: pl.BlockSpec((tm, w), lambda i: (i, 0))
    const = lambda r, c: pl.BlockSpec((r, c), lambda i: (0, 0))
    return pl.pallas_call(
        _out_proj_kernel,
        grid=(n // tm,),
        in_specs=[row(512), row(512), row(D_MODEL), const(512, D_MODEL), const(512, D_MODEL),
                  const(1, D_MODEL), const(1, D_MODEL)],
        out_specs=[row(D_MODEL), row(D_MODEL)],
        out_shape=[jax.ShapeDtypeStruct((n, D_MODEL), F32), jax.ShapeDtypeStruct((n, D_MODEL), BF16)],
        compiler_params=_cparams(("parallel",)),
        name="out_proj",
    )(oa, ob, x2, wo_a, wo_b, g, b)


def _top_values(s, count, with_rank):
    vals = []
    rank = jnp.full(s.shape, float(count), F32) if with_rank else None
    cur = s
    for j in range(count):
        m = jnp.max(cur, axis=0, keepdims=True)
        vals.append(m)
        below = cur < m
        cur = jnp.where(below, cur, -jnp.inf)
        if with_rank:
            rank = jnp.where(below, rank, jnp.minimum(rank, float(j)))
    return vals, rank


def _oddeven_merge_sort_pairs(n):
    pairs = []

    def merge(lo, hi, r):
        step = r * 2
        if step < hi - lo:
            merge(lo, hi, step)
            merge(lo + r, hi, step)
            pairs.extend((i, i + r) for i in range(lo + r, hi - r, step))
        else:
            pairs.append((lo, lo + r))

    def sort(lo, hi):
        if hi - lo >= 1:
            mid = lo + (hi - lo) // 2
            sort(lo, mid)
            sort(mid + 1, hi)
            merge(lo, hi, 1)

    sort(0, n - 1)
    return pairs


def _sorted_top16(s):
    n = PEER_NKEYS // SUBLANES
    x = [s[i * SUBLANES:(i + 1) * SUBLANES, :] for i in range(n)]
    for i, j in _oddeven_merge_sort_pairs(n):
        x[i], x[j] = jnp.maximum(x[i], x[j]), jnp.minimum(x[i], x[j])
    for shift in (4, 2, 1):
        y = [jnp.maximum(x[i], pltpu.roll(x[n - 1 - i], shift, 0)) for i in range(n)]
        for stride in (8, 4, 2, 1):
            for i in range(n):
                if i & stride == 0:
                    y[i], y[i + stride] = jnp.maximum(y[i], y[i + stride]), jnp.minimum(y[i], y[i + stride])
        x = y
    return x


def _peer_route_kernel(h_ref, wqt_ref, sk_ref, r2_ref, p2_ref, na_ref, p1_ref):
    qt = lax.dot_general(wqt_ref[...], h_ref[...], _NT, preferred_element_type=F32)
    half = PEER_DKEY // 2
    for h in range(PEER_HEADS):
        s = []
        for p in range(2):
            idx = h * 2 + p
            qhp = qt[idx * half:(idx + 1) * half, :].astype(BF16)
            s.append(jnp.dot(sk_ref[idx], qhp, preferred_element_type=F32))
        s1, s2 = s
        v1 = [x[0:1] for x in _sorted_top16(s1)]
        top2 = _sorted_top16(s2)
        v2 = [x[0:1] for x in top2]
        rank2 = []
        for i in range(PEER_NKEYS // SUBLANES):
            blk = s2[i * SUBLANES:(i + 1) * SUBLANES, :]
            r = jnp.where(blk < top2[0], 1.0, 0.0)
            for j in range(1, PEER_TOPK):
                r = r + jnp.where(blk < top2[j], 1.0, 0.0)
            rank2.append(r)
        rank2 = jnp.concatenate(rank2, axis=0)
        v2_all = jnp.concatenate(v2, axis=0)
        sub = lax.broadcasted_iota(I32, (8, v2_all.shape[1]), 0)
        cands = [v1[0] + v2_all]
        for i in range(1, 8):
            lim = PEER_TOPK // (i + 1)
            cands.append(jnp.where(sub < lim, v1[i] + v2_all[:8], -jnp.inf))
        cands.append(jnp.concatenate(v1[8:], axis=0) + v2[0])
        cand = jnp.concatenate(cands, axis=0)
        tops, _ = _top_values(cand, PEER_TOPK, False)
        tau = tops[-1]
        tau = jnp.where(tau > -jnp.inf, tau, jnp.min(jnp.where(cand > -jnp.inf, cand, jnp.inf), axis=0,
                                                     keepdims=True))
        mx = v1[0] + v2[0]
        z = jnp.sum(jnp.where(cand >= tau, jnp.exp(cand - mx), 0.0), axis=0, keepdims=True)
        na = jnp.zeros(s1.shape, F32)
        for j in range(8):
            na = na + jnp.where((s1 + v2[j]) >= tau, 1.0, 0.0)
        n_top = jnp.sum(jnp.where((v1[0] + v2_all[8:]) >= tau, 1.0, 0.0), axis=0, keepdims=True)
        na = na + jnp.where(s1 == v1[0], n_top, 0.0)
        r2_ref[h] = pltpu.bitcast(rank2.astype(BF16), jnp.uint32)
        p2_ref[h] = pltpu.bitcast(jnp.exp(s2 - v2[0]).astype(BF16), jnp.uint32)
        na_ref[h] = na
        p1_ref[h] = jnp.exp(s1 - v1[0]) / z


def _peer_route(hb, wqt, sk, tb):
    n = hb.shape[0]
    word_rows = PEER_NKEYS * jnp.dtype(BF16).itemsize // 4
    tab16 = jax.ShapeDtypeStruct((PEER_HEADS, word_rows, n), jnp.uint32)
    tab32 = jax.ShapeDtypeStruct((PEER_HEADS, PEER_NKEYS, n), F32)
    spec16 = pl.BlockSpec((PEER_HEADS, word_rows, tb), lambda i: (0, 0, i))
    spec32 = pl.BlockSpec((PEER_HEADS, PEER_NKEYS, tb), lambda i: (0, 0, i))
    return pl.pallas_call(
        _peer_route_kernel,
        grid=(n // tb,),
        in_specs=[pl.BlockSpec((tb, D_MODEL), lambda i: (i, 0)),
                  pl.BlockSpec(wqt.shape, lambda i: (0, 0)),
                  pl.BlockSpec(sk.shape, lambda i: (0, 0, 0))],
        out_specs=[spec16, spec16, spec32, spec32],
        out_shape=[tab16, tab16, tab32, tab32],
        compiler_params=_cparams(("parallel",)),
        name="peer_route",
    )(hb, wqt, sk)


def _peer_ffn_kernel(hb_ref, h_ref, u_ref, vt_ref, r2_ref, p2_ref, na_ref, p1_ref, g_ref, b_ref,
                     o_ref, acc_ref, w_ref, *, tb, eb, sub):
    j = pl.program_id(1)

    @pl.when(j == 0)
    def _():
        acc_ref[...] = jnp.zeros(acc_ref.shape, F32)

    groups = PEER_NKEYS // BF16_ROWS
    wr = r2_ref.shape[1] // groups
    zero = jnp.zeros((), BF16)
    for sb in range(eb // sub):
        srows = slice(sb * sub, (sb + 1) * sub)
        act_t = lax.dot_general(u_ref[srows, :], hb_ref[...], _NT, preferred_element_type=F32)
        for asub in range(sub // PEER_NKEYS):
            al = sb * (sub // PEER_NKEYS) + asub
            for tt in range(tb // LANES):
                cols = slice(tt * LANES, (tt + 1) * LANES)
                gate = [jnp.zeros((BF16_ROWS, LANES), BF16) for _ in range(groups)]
                for h in range(PEER_HEADS):
                    na = jnp.broadcast_to(na_ref[h, al:al + 1, cols], (BF16_ROWS, LANES)).astype(BF16)
                    p1 = jnp.broadcast_to(p1_ref[h, al:al + 1, cols], (BF16_ROWS, LANES)).astype(BF16)
                    for g in range(groups):
                        ws = slice(g * wr, (g + 1) * wr)
                        r2 = pltpu.bitcast(r2_ref[h, ws, cols], BF16)
                        p2 = pltpu.bitcast(p2_ref[h, ws, cols], BF16)
                        gate[g] = gate[g] + jnp.where(r2 < na, p2, zero) * p1
                for g in range(groups):
                    lrows = slice(asub * PEER_NKEYS + g * BF16_ROWS, asub * PEER_NKEYS + (g + 1) * BF16_ROWS)
                    x = act_t[lrows, cols]
                    gelu = 0.5 * x * (1.0 + lax.erf(x * (2.0 ** -0.5)))
                    w_ref[sb, lrows, cols] = gelu.astype(BF16) * gate[g]
    w_all = w_ref[...].reshape(eb, tb)
    acc_ref[...] += jnp.dot(vt_ref[...], w_all, preferred_element_type=F32)

    @pl.when(j == pl.num_programs(1) - 1)
    def _():
        z = ALPHA * h_ref[...] + acc_ref[...].T
        o_ref[...] = _layer_norm_rows(z, g_ref[...], b_ref[...])


def _peer_ffn(hb, h, u, vt, tabs, g, b, tb, eb):
    n = hb.shape[0]
    ne = u.shape[0]
    tab_spec = pl.BlockSpec((PEER_HEADS, tabs[0].shape[1], tb), lambda i, j: (0, 0, i))
    a_spec = pl.BlockSpec((PEER_HEADS, eb // PEER_NKEYS, tb), lambda i, j: (0, j, i))
    sub = 512
    kernel = functools.partial(_peer_ffn_kernel, tb=tb, eb=eb, sub=sub)
    return pl.pallas_call(
        kernel,
        grid=(n // tb, ne // eb),
        in_specs=[pl.BlockSpec((tb, D_MODEL), lambda i, j: (i, 0)),
                  pl.BlockSpec((tb, D_MODEL), lambda i, j: (i, 0)),
                  pl.BlockSpec((eb, D_MODEL), lambda i, j: (j, 0)),
                  pl.BlockSpec((D_MODEL, eb), lambda i, j: (0, j)),
                  tab_spec, tab_spec, a_spec, a_spec,
                  pl.BlockSpec((1, D_MODEL), lambda i, j: (0, 0)),
                  pl.BlockSpec((1, D_MODEL), lambda i, j: (0, 0))],
        out_specs=pl.BlockSpec((tb, D_MODEL), lambda i, j: (i, 0)),
        out_shape=jax.ShapeDtypeStruct((n, D_MODEL), F32),
        scratch_shapes=[pltpu.VMEM((D_MODEL, tb), F32), pltpu.VMEM((eb // sub, sub, tb), BF16)],
        compiler_params=_cparams(("parallel", "arbitrary")),
        name="peer_ffn",
    )(hb, h, u, vt, *tabs, g, b)


def _tiles(n_tokens, seq):
    tm = 256 if seq % 256 == 0 else BLOCK
    ck = 512 if seq >= 2048 else BLOCK
    tb_route = 256 if n_tokens % 256 == 0 else BLOCK
    tb_ffn = 512 if n_tokens % 512 == 0 else BLOCK
    eb = 2048
    return tm, ck, tb_route, tb_ffn, eb


def kernel(x, w_in, idx_k_g, idx_k_b, sinks, w_o, ln1_g, ln1_b, peer_wq, peer_subkeys, peer_u, peer_v,
           ln2_g, ln2_b):
    batch, seq, _ = x.shape
    n = batch * seq
    tm, ck, tb_route, tb_ffn, eb = _tiles(n, seq)

    inv = ROPE_THETA ** (-jnp.arange(0, HEAD_DIM, 2, dtype=F32) / HEAD_DIM)
    ang = jnp.arange(seq, dtype=F32)[:, None] * inv[None, :]
    cos, sin = jnp.cos(ang), jnp.sin(ang)
    cos_t = jnp.concatenate([cos, cos, cos, cos], axis=-1)
    sin_t = jnp.concatenate([-sin, sin, -sin, sin], axis=-1)

    h = x.reshape(n, D_MODEL)
    for l in range(DEPTH):
        w_pad = jnp.pad(w_in[l], ((0, 0), (0, MIX_COLS_PAD - MIX_COLS))).astype(BF16)
        kg = jnp.pad(idx_k_g[l], (0, LANES - IDX_DIM)).reshape(1, LANES)
        kb = jnp.pad(idx_k_b[l], (0, LANES - IDX_DIM)).reshape(1, LANES)
        qa, ka, va, qb, kbb, vb, qi, ki, wi = _in_proj(h, w_pad, cos_t, sin_t, kg, kb, seq, tm)
        oa = _swa(sinks[l], qa, ka, va, batch, seq)
        ob = _dsa(qb, qi, wi, ki, kbb, vb, batch, seq, ck)
        wo = w_o[l].astype(BF16)
        h1, h1b = _out_proj(oa, ob, h, wo[:512], wo[512:], ln1_g[l].reshape(1, -1), ln1_b[l].reshape(1, -1), tm)
        wqt = peer_wq[l].T.astype(BF16)
        sk = peer_subkeys[l].reshape(PEER_HEADS * 2, PEER_NKEYS, PEER_DKEY // 2).astype(BF16)
        tabs = _peer_route(h1b, wqt, sk, tb_route)
        u = peer_u[l].astype(BF16)
        vt = peer_v[l].T.astype(BF16)
        h = _peer_ffn(h1b, h1, u, vt, tabs, ln2_g[l].reshape(1, -1), ln2_b[l].reshape(1, -1), tb_ffn, eb)
    return h.reshape(batch, seq, D_MODEL)
```

```python
import functools
import math

import jax
import jax.numpy as jnp
from jax import lax
from jax.experimental import pallas as pl
from jax.experimental.pallas import tpu as pltpu

F32 = jnp.float32
BF16 = jnp.bfloat16
I32 = jnp.int32

D_MODEL = 1024
HEAD_DIM = 64
SWA_HEADS = 8
SWA_KV_HEADS = 2
WINDOW = 128
BLOCK = 128
DSA_HEADS = 8
IDX_HEADS = 4
IDX_DIM = 64
TOPK_MAX = 256
ROPE_THETA = 10000.0
PEER_HEADS = 8
PEER_NKEYS = 128
PEER_DKEY = 256
PEER_TOPK = 16
LN_EPS = 1e-5
DEPTH = 1
ALPHA = (2.0 * DEPTH) ** 0.25

LANES = 128
SUBLANES = 8
BF16_ROWS = 16
VMEM_LIMIT_BYTES = 56 * 1024 * 1024

MIX_COLS = 1732
MIX_COLS_PAD = 1792
WI_LANE0 = 64

NEG_BIG = -1e30
INT_MIN = -(2 ** 31)

_NT = (((1,), (1,)), ((), ()))


def _cparams(sem):
    return pltpu.CompilerParams(dimension_semantics=sem, vmem_limit_bytes=VMEM_LIMIT_BYTES)


def _layer_norm_rows(z, g, b):
    mu = jnp.mean(z, axis=-1, keepdims=True)
    zc = z - mu
    var = jnp.mean(zc * zc, axis=-1, keepdims=True)
    return zc * lax.rsqrt(var + LN_EPS) * g + b


def _in_proj_kernel(x_ref, w_ref, cos_ref, sin_ref, kg_ref, kb_ref,
                    qa_ref, ka_ref, va_ref, qb_ref, kb_out_ref, vb_ref, qi_ref, ki_ref, wi_ref):
    xb = x_ref[...].astype(BF16)
    proj = jnp.dot(xb, w_ref[...], preferred_element_type=F32)
    cos = cos_ref[...]
    sin = sin_ref[...]
    lane = lax.broadcasted_iota(I32, cos.shape, 1)
    first_half = (lane % HEAD_DIM) < (HEAD_DIM // 2)

    def rope(t):
        swapped = jnp.where(first_half, pltpu.roll(t, LANES - HEAD_DIM // 2, 1),
                            pltpu.roll(t, HEAD_DIM // 2, 1))
        return t * cos + swapped * sin

    def slab(i):
        return proj[:, i * LANES:(i + 1) * LANES]

    q_scale = HEAD_DIM ** -0.5
    for i in range(4):
        qa_ref[:, i * LANES:(i + 1) * LANES] = (rope(slab(i)) * q_scale).astype(BF16)
        qb_ref[:, i * LANES:(i + 1) * LANES] = (rope(slab(6 + i)) * q_scale).astype(BF16)
    ka_ref[...] = rope(slab(4)).astype(BF16)
    va_ref[...] = slab(5).astype(BF16)
    kv = slab(10)
    kv_r = rope(kv)
    kb_out_ref[...] = kv_r[:, :HEAD_DIM].astype(BF16)
    vb_ref[...] = jnp.where(lane < HEAD_DIM, pltpu.roll(kv, HEAD_DIM, 1),
                            jnp.where(lane == HEAD_DIM, 1.0, 0.0)).astype(BF16)
    for i in range(2):
        qi_ref[:, i * LANES:(i + 1) * LANES] = rope(slab(11 + i)).astype(BF16)
    last = slab(13)
    is_ki = lane < IDX_DIM
    mu = jnp.sum(jnp.where(is_ki, last, 0.0), axis=-1, keepdims=True) * (1.0 / IDX_DIM)
    cen = last - mu
    var = jnp.sum(jnp.where(is_ki, cen * cen, 0.0), axis=-1, keepdims=True) * (1.0 / IDX_DIM)
    kin = cen * lax.rsqrt(var + LN_EPS) * kg_ref[...] + kb_ref[...]
    kin = jnp.where(is_ki, kin, 0.0)
    ki_ref[...] = rope(kin)[:, :IDX_DIM].astype(BF16)
    w_scale = (IDX_HEADS ** -0.5) * (IDX_DIM ** -0.5)
    wi_ref[...] = last * w_scale


def _in_proj(x2, w_pad, cos_t, sin_t, kg, kb, seq, tm):
    n = x2.shape[0]
    nt = n // tm
    tps = seq // tm
    row = lambda w: pl.BlockSpec((tm, w), lambda i: (i, 0))
    out_shapes = [
        jax.ShapeDtypeStruct((n, 512), BF16), jax.ShapeDtypeStruct((n, 128), BF16),
        jax.ShapeDtypeStruct((n, 128), BF16), jax.ShapeDtypeStruct((n, 512), BF16),
        jax.ShapeDtypeStruct((n, 64), BF16), jax.ShapeDtypeStruct((n, 128), BF16),
        jax.ShapeDtypeStruct((n, 256), BF16), jax.ShapeDtypeStruct((n, 64), BF16),
        jax.ShapeDtypeStruct((n, 128), F32),
    ]
    return pl.pallas_call(
        _in_proj_kernel,
        grid=(nt,),
        in_specs=[
            row(D_MODEL),
            pl.BlockSpec((D_MODEL, MIX_COLS_PAD), lambda i: (0, 0)),
            pl.BlockSpec((tm, LANES), lambda i: (i % tps, 0)),
            pl.BlockSpec((tm, LANES), lambda i: (i % tps, 0)),
            pl.BlockSpec((1, LANES), lambda i: (0, 0)),
            pl.BlockSpec((1, LANES), lambda i: (0, 0)),
        ],
        out_specs=[row(512), row(128), row(128), row(512), row(64), row(128), row(256), row(64), row(128)],
        out_shape=out_shapes,
        compiler_params=_cparams(("parallel",)),
        name="in_proj",
    )(x2, w_pad, cos_t, sin_t, kg, kb)


def _swa_kernel(sink_ref, q_ref, kp_ref, kc_ref, vp_ref, vc_ref, o_ref):
    n = pl.program_id(1)
    qi = lax.broadcasted_iota(I32, (BLOCK, 2 * BLOCK), 0)
    sj = lax.broadcasted_iota(I32, (BLOCK, 2 * BLOCK), 1)
    diff = qi + BLOCK - sj
    valid = (diff >= 0) & (diff < WINDOW) & ((n > 0) | (sj >= BLOCK))
    group = SWA_HEADS // SWA_KV_HEADS
    for kh in range(SWA_KV_HEADS):
        cs = slice(kh * HEAD_DIM, (kh + 1) * HEAD_DIM)
        k = jnp.concatenate([kp_ref[:, cs], kc_ref[:, cs]], axis=0)
        v = jnp.concatenate([vp_ref[:, cs], vc_ref[:, cs]], axis=0)
        for g in range(group):
            h = kh * group + g
            hs = slice(h * HEAD_DIM, (h + 1) * HEAD_DIM)
            s = lax.dot_general(q_ref[:, hs], k, _NT, preferred_element_type=F32)
            s = jnp.where(valid, s, NEG_BIG)
            sink = sink_ref[h]
            m = jnp.maximum(jnp.max(s, axis=-1, keepdims=True), sink)
            p = jnp.exp(s - m)
            denom = jnp.sum(p, axis=-1, keepdims=True) + jnp.exp(sink - m)
            o = jnp.dot(p.astype(BF16), v, preferred_element_type=F32)
            o_ref[:, hs] = (o / denom).astype(o_ref.dtype)


def _swa(sinks, qa, ka, va, batch, seq):
    nb = seq // BLOCK
    cur = lambda w: pl.BlockSpec((BLOCK, w), lambda b, n: (b * nb + n, 0))
    prev = lambda w: pl.BlockSpec((BLOCK, w), lambda b, n: (b * nb + jnp.maximum(n - 1, 0), 0))
    return pl.pallas_call(
        _swa_kernel,
        grid=(batch, nb),
        in_specs=[pl.BlockSpec(memory_space=pltpu.SMEM), cur(512), prev(128), cur(128), prev(128), cur(128)],
        out_specs=cur(512),
        out_shape=jax.ShapeDtypeStruct((batch * seq, 512), BF16),
        compiler_params=_cparams(("parallel", "parallel")),
        name="swa",
    )(sinks, qa, ka, ka, va, va)


DSA_ROWS = 2 * BLOCK
DIGIT_BITS = 14


def _dsa_kernel(qb_ref, qi_ref, wi_ref, ki_ref, kb_ref, vb_ref, o_ref,
                keys_ref, keyt_ref, dig1_ref, dig2_ref, vst_ref, tie_ref, bias_ref, m_ref, acc_ref, qs_ref,
                s_ref, p_ref, alpha_ref,
                *, ck, nc, k_sel, seq_bits):
    n = pl.program_id(1)
    t0 = n * DSA_ROWS
    nck = (t0 + DSA_ROWS + ck - 1) // ck
    lane_tiles = ck // LANES
    qpos = t0 + lax.broadcasted_iota(I32, (DSA_ROWS, ck), 0)
    lane_pos = lax.broadcasted_iota(I32, (DSA_ROWS, ck), 1)

    qi = qi_ref[...]
    qi4 = jnp.concatenate([qi[:, h * IDX_DIM:(h + 1) * IDX_DIM] for h in range(IDX_HEADS)], axis=0)
    wi = wi_ref[...]
    wcols = [wi[:, WI_LANE0 + h:WI_LANE0 + h + 1] for h in range(IDX_HEADS)]

    def score_chunk(c, causal):
        r = lax.dot_general(qi4, ki_ref[c], _NT, preferred_element_type=F32)
        r = jnp.maximum(r, 0.0)
        isc = wcols[0] * r[0:DSA_ROWS]
        for h in range(1, IDX_HEADS):
            isc = isc + wcols[h] * r[h * DSA_ROWS:(h + 1) * DSA_ROWS]
        isc = jnp.where(isc == 0.0, 0.0, isc)
        bits = pltpu.bitcast(isc, I32)
        key = bits ^ ((bits >> 31) & 0x7FFFFFFF)
        if causal:
            key = jnp.where(c * ck + lane_pos <= qpos, key, INT_MIN)
        keys_ref[c] = key
        key_t = pltpu.bitcast(pltpu.bitcast(key, F32).T, I32)
        keyt_ref[c] = key_t
        key_u = key_t ^ INT_MIN
        dig1_ref[c] = pltpu.bitcast(digit_pattern(lax.shift_right_logical(key_u, 32 - DIGIT_BITS)), jnp.uint32)
        dig2_ref[c] = pltpu.bitcast(digit_pattern(lax.shift_right_logical(key_u, 32 - 2 * DIGIT_BITS)
                                                  & (2 ** DIGIT_BITS - 1)), jnp.uint32)

    def digit_pattern(d):
        return pltpu.bitcast(lax.shift_left(d + 128, 16), F32).astype(BF16)

    def scan_chunks(lo, hi, causal):
        def body(c, carry):
            score_chunk(c, causal)
            return carry
        lax.fori_loop(lo, hi, body, 0)

    n_past = (t0 + 1) // ck
    scan_chunks(0, n_past, False)
    scan_chunks(n_past, nck, True)

    sub_pos = lax.broadcasted_iota(I32, (SUBLANES, LANES), 0)
    groups = DSA_ROWS // LANES

    def to_rows(x):
        wide = jnp.broadcast_to(pltpu.bitcast(x[0:1, :], F32), (LANES, LANES))
        return pltpu.bitcast(wide.T, I32)

    def select(v):
        def count(g, pred):
            parts = [jnp.zeros((SUBLANES, LANES), I32) for _ in range(4)]
            i = 0
            for c in range(v):
                for kb in range(ck // SUBLANES):
                    tile = keyt_ref[c, kb * SUBLANES:(kb + 1) * SUBLANES, g * LANES:(g + 1) * LANES]
                    hit = pred(tile, c * ck + kb * SUBLANES + sub_pos)
                    parts[i % 4] = parts[i % 4] + jnp.where(hit, 1, 0)
                    i += 1
            tot = jnp.sum((parts[0] + parts[1]) + (parts[2] + parts[3]), axis=0, keepdims=True)
            return jnp.broadcast_to(tot, (SUBLANES, LANES))

        def bit_step(i, ts):
            bit = lax.shift_left(jnp.int32(1), 31 - i)
            out = []
            for g in range(groups):
                cand = ts[g] | bit
                cand_s = cand ^ INT_MIN
                out.append(jnp.where(count(g, lambda key, pos: key >= cand_s) >= k_sel, cand, ts[g]))
            return tuple(out)

        prow = SUBLANES * 4 // jnp.dtype(BF16).itemsize
        wrow = SUBLANES
        one16 = jnp.ones((), BF16)
        zero16 = jnp.zeros((), BF16)

        def pattern_rows(code):
            return digit_pattern(jnp.broadcast_to(code[0:1, :], (prow, LANES)))

        def count_packed(ref, g, pred):
            parts = [jnp.zeros((prow, LANES), BF16) for _ in range(4)]
            i = 0
            for c in range(v):
                for kb in range(ref.shape[1] // wrow):
                    tile = pltpu.bitcast(ref[c, kb * wrow:(kb + 1) * wrow, g * LANES:(g + 1) * LANES], BF16)
                    parts[i % 4] = parts[i % 4] + jnp.where(pred(tile), one16, zero16)
                    i += 1
            tot = jnp.sum(((parts[0] + parts[1]) + (parts[2] + parts[3])).astype(F32), axis=0, keepdims=True)
            return jnp.broadcast_to(tot, (SUBLANES, LANES))

        def digit_search(ref, base):
            def step(i, ds):
                bit = lax.shift_left(jnp.int32(1), DIGIT_BITS - 1 - i)
                out = []
                for g in range(groups):
                    cand = ds[g] | bit
                    pat = pattern_rows(cand)
                    cnt = base[g] + count_packed(ref, g, lambda t: t >= pat)
                    out.append(jnp.where(cnt >= k_sel, cand, ds[g]))
                return tuple(out)
            return lax.fori_loop(0, DIGIT_BITS, step, (zi,) * groups)

        zi = jnp.zeros((SUBLANES, LANES), I32)
        zf = jnp.zeros((SUBLANES, LANES), F32)
        d1 = digit_search(dig1_ref, (zf,) * groups)
        above, d2_base = [], []
        for g in range(groups):
            pat1 = pattern_rows(d1[g])
            d2_base.append(count_packed(dig1_ref, g, lambda t: t > pat1))
            for c in range(v):
                for kb in range(dig1_ref.shape[1] // wrow):
                    rows = slice(kb * wrow, (kb + 1) * wrow)
                    cols = slice(g * LANES, (g + 1) * LANES)
                    same = pltpu.bitcast(dig1_ref[c, rows, cols], BF16) == pat1
                    kept = jnp.where(same, pltpu.bitcast(dig2_ref[c, rows, cols], BF16), zero16)
                    dig2_ref[c, rows, cols] = pltpu.bitcast(kept, jnp.uint32)
        d2 = digit_search(dig2_ref, tuple(d2_base))
        ts = tuple(lax.shift_left(d1[g], 32 - DIGIT_BITS) | lax.shift_left(d2[g], 32 - 2 * DIGIT_BITS)
                   for g in range(groups))
        ts = lax.fori_loop(2 * DIGIT_BITS, 32, bit_step, ts)
        vstar = [t ^ INT_MIN for t in ts]
        need, surplus = [], []
        for g in range(groups):
            vs = vstar[g]
            vst_ref[g * LANES:(g + 1) * LANES, :] = to_rows(vs)
            tie_ref[g * LANES:(g + 1) * LANES, :] = jnp.full((LANES, LANES), 2 ** seq_bits, I32)
            cnt_gt = count(g, lambda key, pos: key > vs)
            cnt_eq = count(g, lambda key, pos: key == vs)
            need.append(k_sel - cnt_gt)
            surplus.append(jnp.max(cnt_eq - need[g]))

        @pl.when(functools.reduce(jnp.maximum, surplus) > 0)
        def _():
            for g in range(groups):
                vs = vstar[g]

                def idx_step(i, p):
                    cand = p | lax.shift_left(jnp.int32(1), seq_bits - 1 - i)
                    cnt = count(g, lambda key, pos: (key == vs) & (pos < cand))
                    return jnp.where(cnt < need[g], cand, p)

                tie_ref[g * LANES:(g + 1) * LANES, :] = to_rows(lax.fori_loop(0, seq_bits, idx_step, zi))

    for v in range(1, nc + 1):
        pl.when(nck == v)(functools.partial(select, v))

    qb = qb_ref[...]
    for h in range(DSA_HEADS):
        qs_ref[h * DSA_ROWS:(h + 1) * DSA_ROWS, :] = qb[:, h * HEAD_DIM:(h + 1) * HEAD_DIM]
    m_ref[...] = jnp.full(m_ref.shape, NEG_BIG, F32)
    acc_ref[...] = jnp.zeros(acc_ref.shape, F32)
    row_lane = lax.broadcasted_iota(I32, (DSA_ROWS, LANES), 1)

    def stage_a(it, par):
        ca = jnp.minimum(it, nck - 1)
        s_ref[par] = lax.dot_general(qs_ref[...], kb_ref[ca], _NT, preferred_element_type=F32)

    def stage_b(it, par):
        cb = it - 1
        vst = vst_ref[...]
        tie = tie_ref[...]
        for j in range(lane_tiles):
            key = keys_ref[cb, :, j * LANES:(j + 1) * LANES]
            pos = cb * ck + j * LANES + row_lane
            sel = ((key > vst) | ((key == vst) & (pos <= tie))) & (key > INT_MIN)
            bias_ref[:, j * LANES:(j + 1) * LANES] = jnp.where(sel, 0.0, NEG_BIG)
        sb = 1 - par
        for h in range(DSA_HEADS):
            rows = slice(h * DSA_ROWS, (h + 1) * DSA_ROWS)
            s = s_ref[sb, rows, :] + bias_ref[...]
            s_ref[sb, rows, :] = s
            part = s[:, 0:LANES]
            for j in range(1, lane_tiles):
                part = jnp.maximum(part, s[:, j * LANES:(j + 1) * LANES])
            m_old = m_ref[h]
            m_new = jnp.maximum(m_old, jnp.max(part, axis=-1, keepdims=True))
            alpha_ref[sb, h] = jnp.exp(m_old - m_new)
            m_ref[h] = m_new
        for h in range(DSA_HEADS):
            rows = slice(h * DSA_ROWS, (h + 1) * DSA_ROWS)
            p_ref[sb, rows, :] = jnp.exp(s_ref[sb, rows, :] - m_ref[h]).astype(BF16)

    def stage_c(it, par):
        cc = jnp.maximum(it - 2, 0)
        pv = jnp.dot(p_ref[par], vb_ref[cc], preferred_element_type=F32)
        for h in range(DSA_HEADS):
            acc_ref[h] = alpha_ref[par, h] * acc_ref[h] + pv[h * DSA_ROWS:(h + 1) * DSA_ROWS]

    def attend_step(it, carry):
        for par in range(2):
            @pl.when(it % 2 == par)
            def _():
                stage_a(it, par)
                stage_b(it, par)
                stage_c(it, par)
        return carry

    p_ref[1] = jnp.zeros(p_ref.shape[1:], BF16)
    alpha_ref[1] = jnp.ones(alpha_ref.shape[1:], F32)
    stage_a(0, 0)
    lax.fori_loop(1, nck + 1, attend_step, 0)
    for par in range(2):
        pl.when((nck + 1) % 2 == par)(functools.partial(stage_c, nck + 1, par))
    for h in range(DSA_HEADS):
        a = acc_ref[h]
        o_ref[:, h * HEAD_DIM:(h + 1) * HEAD_DIM] = (a[:, :HEAD_DIM] / a[:, HEAD_DIM:HEAD_DIM + 1]).astype(o_ref.dtype)


def _dsa(qb, qi, wi, ki, kb, vb, batch, seq, ck):
    nq = seq // DSA_ROWS
    nc = seq // ck
    k_sel = min(TOPK_MAX, seq // 4)
    seq_bits = int(math.log2(seq))
    assert 2 ** seq_bits == seq and k_sel <= ck and seq % DSA_ROWS == 0
    assert seq // BF16_ROWS <= 256
    chunked = lambda a: a.reshape(batch, nc, ck, a.shape[-1])
    blk = lambda w: pl.BlockSpec((DSA_ROWS, w), lambda b, n: (b * nq + n, 0))
    whole = lambda w: pl.BlockSpec((None, nc, ck, w), lambda b, n: (b, 0, 0, 0))
    kernel = functools.partial(_dsa_kernel, ck=ck, nc=nc, k_sel=k_sel, seq_bits=seq_bits)
    return pl.pallas_call(
        kernel,
        grid=(batch, nq),
        in_specs=[blk(512), blk(256), blk(128), whole(IDX_DIM), whole(HEAD_DIM), whole(LANES)],
        out_specs=blk(512),
        out_shape=jax.ShapeDtypeStruct((batch * seq, 512), BF16),
        scratch_shapes=[
            pltpu.VMEM((nc, DSA_ROWS, ck), I32),
            pltpu.VMEM((nc, ck, DSA_ROWS), I32),
            pltpu.VMEM((nc, ck * jnp.dtype(BF16).itemsize // 4, DSA_ROWS), jnp.uint32),
            pltpu.VMEM((nc, ck * jnp.dtype(BF16).itemsize // 4, DSA_ROWS), jnp.uint32),
            pltpu.VMEM((DSA_ROWS, LANES), I32),
            pltpu.VMEM((DSA_ROWS, LANES), I32),
            pltpu.VMEM((DSA_ROWS, ck), F32),
            pltpu.VMEM((DSA_HEADS, DSA_ROWS, 1), F32),
            pltpu.VMEM((DSA_HEADS, DSA_ROWS, LANES), F32),
            pltpu.VMEM((DSA_HEADS * DSA_ROWS, HEAD_DIM), BF16),
            pltpu.VMEM((2, DSA_HEADS * DSA_ROWS, ck), F32),
            pltpu.VMEM((2, DSA_HEADS * DSA_ROWS, ck), BF16),
            pltpu.VMEM((2, DSA_HEADS, DSA_ROWS, 1), F32),
        ],
        compiler_params=_cparams(("parallel", "arbitrary")),
        name="dsa",
    )(qb, qi, wi, chunked(ki), chunked(kb), chunked(vb))


def _out_proj_kernel(oa_ref, ob_ref, x_ref, wa_ref, wb_ref, g_ref, b_ref, h_ref, hb_ref):
    mix = jnp.dot(oa_ref[...], wa_ref[...], preferred_element_type=F32)
    mix = mix + jnp.dot(ob_ref[...], wb_ref[...], preferred_element_type=F32)
    h = _layer_norm_rows(ALPHA * x_ref[...] + mix, g_ref[...], b_ref[...])
    h_ref[...] = h
    hb_ref[...] = h.astype(BF16)


def _out_proj(oa, ob, x2, wo_a, wo_b, g, b, tm):
    n = x2.shape[0]
    row = lambda w: pl.BlockSpec((tm, w), lambda i: (i, 0))
    const = lambda r, c: pl.BlockSpec((r, c), lambda i: (0, 0))
    return pl.pallas_call(
        _out_proj_kernel,
        grid=(n // tm,),
        in_specs=[row(512), row(512), row(D_MODEL), const(512, D_MODEL), const(512, D_MODEL),
                  const(1, D_MODEL), const(1, D_MODEL)],
        out_specs=[row(D_MODEL), row(D_MODEL)],
        out_shape=[jax.ShapeDtypeStruct((n, D_MODEL), F32), jax.ShapeDtypeStruct((n, D_MODEL), BF16)],
        compiler_params=_cparams(("parallel",)),
        name="out_proj",
    )(oa, ob, x2, wo_a, wo_b, g, b)


def _top_values(s, count, with_rank):
    vals = []
    rank = jnp.full(s.shape, float(count), F32) if with_rank else None
    cur = s
    for j in range(count):
        m = jnp.max(cur, axis=0, keepdims=True)
        vals.append(m)
        below = cur < m
        cur = jnp.where(below, cur, -jnp.inf)
        if with_rank:
            rank = jnp.where(below, rank, jnp.minimum(rank, float(j)))
    return vals, rank


def _oddeven_merge_sort_pairs(n):
    pairs = []

    def merge(lo, hi, r):
        step = r * 2
        if step < hi - lo:
            merge(lo, hi, step)
            merge(lo + r, hi, step)
            pairs.extend((i, i + r) for i in range(lo + r, hi - r, step))
        else:
            pairs.append((lo, lo + r))

    def sort(lo, hi):
        if hi - lo >= 1:
            mid = lo + (hi - lo) // 2
            sort(lo, mid)
            sort(mid + 1, hi)
            merge(lo, hi, 1)

    sort(0, n - 1)
    return pairs


def _sorted_top16(s):
    n = PEER_NKEYS // SUBLANES
    x = [s[i * SUBLANES:(i + 1) * SUBLANES, :] for i in range(n)]
    for i, j in _oddeven_merge_sort_pairs(n):
        x[i], x[j] = jnp.maximum(x[i], x[j]), jnp.minimum(x[i], x[j])
    for shift in (4, 2, 1):
        y = [jnp.maximum(x[i], pltpu.roll(x[n - 1 - i], shift, 0)) for i in range(n)]
        for stride in (8, 4, 2, 1):
            for i in range(n):
                if i & stride == 0:
                    y[i], y[i + stride] = jnp.maximum(y[i], y[i + stride]), jnp.minimum(y[i], y[i + stride])
        x = y
    return x


def _peer_route_kernel(h_ref, wqt_ref, sk_ref, r2_ref, p2_ref, na_ref, p1_ref):
    qt = lax.dot_general(wqt_ref[...], h_ref[...], _NT, preferred_element_type=F32)
    half = PEER_DKEY // 2
    for h in range(PEER_HEADS):
        s = []
        for p in range(2):
            idx = h * 2 + p
            qhp = qt[idx * half:(idx + 1) * half, :].astype(BF16)
            s.append(jnp.dot(sk_ref[idx], qhp, preferred_element_type=F32))
        s1, s2 = s
        v1 = [x[0:1] for x in _sorted_top16(s1)]
        top2 = _sorted_top16(s2)
        v2 = [x[0:1] for x in top2]
        rank2 = []
        for i in range(PEER_NKEYS // SUBLANES):
            blk = s2[i * SUBLANES:(i + 1) * SUBLANES, :]
            r = jnp.where(blk < top2[0], 1.0, 0.0)
            for j in range(1, PEER_TOPK):
                r = r + jnp.where(blk < top2[j], 1.0, 0.0)
            rank2.append(r)
        rank2 = jnp.concatenate(rank2, axis=0)
        v2_all = jnp.concatenate(v2, axis=0)
        sub = lax.broadcasted_iota(I32, (8, v2_all.shape[1]), 0)
        cands = [v1[0] + v2_all]
        for i in range(1, 8):
            lim = PEER_TOPK // (i + 1)
            cands.append(jnp.where(sub < lim, v1[i] + v2_all[:8], -jnp.inf))
        cands.append(jnp.concatenate(v1[8:], axis=0) + v2[0])
        cand = jnp.concatenate(cands, axis=0)
        tops, _ = _top_values(cand, PEER_TOPK, False)
        tau = tops[-1]
        tau = jnp.where(tau > -jnp.inf, tau, jnp.min(jnp.where(cand > -jnp.inf, cand, jnp.inf), axis=0,
                                                     keepdims=True))
        mx = v1[0] + v2[0]
        z = jnp.sum(jnp.where(cand >= tau, jnp.exp(cand - mx), 0.0), axis=0, keepdims=True)
        na = jnp.zeros(s1.shape, F32)
        for j in range(8):
            na = na + jnp.where((s1 + v2[j]) >= tau, 1.0, 0.0)
        n_top = jnp.sum(jnp.where((v1[0] + v2_all[8:]) >= tau, 1.0, 0.0), axis=0, keepdims=True)
        na = na + jnp.where(s1 == v1[0], n_top, 0.0)
        r2_ref[h] = pltpu.bitcast(rank2.astype(BF16), jnp.uint32)
        p2_ref[h] = pltpu.bitcast(jnp.exp(s2 - v2[0]).astype(BF16), jnp.uint32)
        na_ref[h] = na
        p1_ref[h] = jnp.exp(s1 - v1[0]) / z


def _peer_route(hb, wqt, sk, tb):
    n = hb.shape[0]
    word_rows = PEER_NKEYS * jnp.dtype(BF16).itemsize // 4
    tab16 = jax.ShapeDtypeStruct((PEER_HEADS, word_rows, n), jnp.uint32)
    tab32 = jax.ShapeDtypeStruct((PEER_HEADS, PEER_NKEYS, n), F32)
    spec16 = pl.BlockSpec((PEER_HEADS, word_rows, tb), lambda i: (0, 0, i))
    spec32 = pl.BlockSpec((PEER_HEADS, PEER_NKEYS, tb), lambda i: (0, 0, i))
    return pl.pallas_call(
        _peer_route_kernel,
        grid=(n // tb,),
        in_specs=[pl.BlockSpec((tb, D_MODEL), lambda i: (i, 0)),
                  pl.BlockSpec(wqt.shape, lambda i: (0, 0)),
                  pl.BlockSpec(sk.shape, lambda i: (0, 0, 0))],
        out_specs=[spec16, spec16, spec32, spec32],
        out_shape=[tab16, tab16, tab32, tab32],
        compiler_params=_cparams(("parallel",)),
        name="peer_route",
    )(hb, wqt, sk)


def _peer_ffn_kernel(hb_ref, h_ref, u_ref, vt_ref, r2_ref, p2_ref, na_ref, p1_ref, g_ref, b_ref,
                     o_ref, acc_ref, w_ref, *, tb, eb, sub):
    j = pl.program_id(1)

    @pl.when(j == 0)
    def _():
        acc_ref[...] = jnp.zeros(acc_ref.shape, F32)

    groups = PEER_NKEYS // BF16_ROWS
    wr = r2_ref.shape[1] // groups
    zero = jnp.zeros((), BF16)
    for sb in range(eb // sub):
        srows = slice(sb * sub, (sb + 1) * sub)
        act_t = lax.dot_general(u_ref[srows, :], hb_ref[...], _NT, preferred_element_type=F32)
        for asub in range(sub // PEER_NKEYS):
            al = sb * (sub // PEER_NKEYS) + asub
            for tt in range(tb // LANES):
                cols = slice(tt * LANES, (tt + 1) * LANES)
                gate = [jnp.zeros((BF16_ROWS, LANES), BF16) for _ in range(groups)]
                for h in range(PEER_HEADS):
                    na = jnp.broadcast_to(na_ref[h, al:al + 1, cols], (BF16_ROWS, LANES)).astype(BF16)
                    p1 = jnp.broadcast_to(p1_ref[h, al:al + 1, cols], (BF16_ROWS, LANES)).astype(BF16)
                    for g in range(groups):
                        ws = slice(g * wr, (g + 1) * wr)
                        r2 = pltpu.bitcast(r2_ref[h, ws, cols], BF16)
                        p2 = pltpu.bitcast(p2_ref[h, ws, cols], BF16)
                        gate[g] = gate[g] + jnp.where(r2 < na, p2, zero) * p1
                for g in range(groups):
                    lrows = slice(asub * PEER_NKEYS + g * BF16_ROWS, asub * PEER_NKEYS + (g + 1) * BF16_ROWS)
                    x = act_t[lrows, cols]
                    gelu = 0.5 * x * (1.0 + lax.erf(x * (2.0 ** -0.5)))
                    w_ref[sb, lrows, cols] = gelu.astype(BF16) * gate[g]
    w_all = w_ref[...].reshape(eb, tb)
    acc_ref[...] += jnp.dot(vt_ref[...], w_all, preferred_element_type=F32)

    @pl.when(j == pl.num_programs(1) - 1)
    def _():
        z = ALPHA * h_ref[...] + acc_ref[...].T
        o_ref[...] = _layer_norm_rows(z, g_ref[...], b_ref[...])


def _peer_ffn(hb, h, u, vt, tabs, g, b, tb, eb):
    n = hb.shape[0]
    ne = u.shape[0]
    tab_spec = pl.BlockSpec((PEER_HEADS, tabs[0].shape[1], tb), lambda i, j: (0, 0, i))
    a_spec = pl.BlockSpec((PEER_HEADS, eb // PEER_NKEYS, tb), lambda i, j: (0, j, i))
    sub = 512
    kernel = functools.partial(_peer_ffn_kernel, tb=tb, eb=eb, sub=sub)
    return pl.pallas_call(
        kernel,
        grid=(n // tb, ne // eb),
        in_specs=[pl.BlockSpec((tb, D_MODEL), lambda i, j: (i, 0)),
                  pl.BlockSpec((tb, D_MODEL), lambda i, j: (i, 0)),
                  pl.BlockSpec((eb, D_MODEL), lambda i, j: (j, 0)),
                  pl.BlockSpec((D_MODEL, eb), lambda i, j: (0, j)),
                  tab_spec, tab_spec, a_spec, a_spec,
                  pl.BlockSpec((1, D_MODEL), lambda i, j: (0, 0)),
                  pl.BlockSpec((1, D_MODEL), lambda i, j: (0, 0))],
        out_specs=pl.BlockSpec((tb, D_MODEL), lambda i, j: (i, 0)),
        out_shape=jax.ShapeDtypeStruct((n, D_MODEL), F32),
        scratch_shapes=[pltpu.VMEM((D_MODEL, tb), F32), pltpu.VMEM((eb // sub, sub, tb), BF16)],
        compiler_params=_cparams(("parallel", "arbitrary")),
        name="peer_ffn",
    )(hb, h, u, vt, *tabs, g, b)


def _tiles(n_tokens, seq):
    tm = 256 if seq % 256 == 0 else BLOCK
    ck = 512 if seq >= 2048 else BLOCK
    tb_route = 256 if n_tokens % 256 == 0 else BLOCK
    tb_ffn = 512 if n_tokens % 512 == 0 else BLOCK
    eb = 2048
    return tm, ck, tb_route, tb_ffn, eb


def kernel(x, w_in, idx_k_g, idx_k_b, sinks, w_o, ln1_g, ln1_b, peer_wq, peer_subkeys, peer_u, peer_v,
           ln2_g, ln2_b):
    batch, seq, _ = x.shape
    n = batch * seq
    tm, ck, tb_route, tb_ffn, eb = _tiles(n, seq)

    inv = ROPE_THETA ** (-jnp.arange(0, HEAD_DIM, 2, dtype=F32) / HEAD_DIM)
    ang = jnp.arange(seq, dtype=F32)[:, None] * inv[None, :]
    cos, sin = jnp.cos(ang), jnp.sin(ang)
    cos_t = jnp.concatenate([cos, cos, cos, cos], axis=-1)
    sin_t = jnp.concatenate([-sin, sin, -sin, sin], axis=-1)

    h = x.reshape(n, D_MODEL)
    for l in range(DEPTH):
        w_pad = jnp.pad(w_in[l], ((0, 0), (0, MIX_COLS_PAD - MIX_COLS))).astype(BF16)
        kg = jnp.pad(idx_k_g[l], (0, LANES - IDX_DIM)).reshape(1, LANES)
        kb = jnp.pad(idx_k_b[l], (0, LANES - IDX_DIM)).reshape(1, LANES)
        qa, ka, va, qb, kbb, vb, qi, ki, wi = _in_proj(h, w_pad, cos_t, sin_t, kg, kb, seq, tm)
        oa = _swa(sinks[l], qa, ka, va, batch, seq)
        ob = _dsa(qb, qi, wi, ki, kbb, vb, batch, seq, ck)
        wo = w_o[l].astype(BF16)
        h1, h1b = _out_proj(oa, ob, h, wo[:512], wo[512:], ln1_g[l].reshape(1, -1), ln1_b[l].reshape(1, -1), tm)
        wqt = peer_wq[l].T.astype(BF16)
        sk = peer_subkeys[l].reshape(PEER_HEADS * 2, PEER_NKEYS, PEER_DKEY // 2).astype(BF16)
        tabs = _peer_route(h1b, wqt, sk, tb_route)
        u = peer_u[l].astype(BF16)
        vt = peer_v[l].T.astype(BF16)
        h = _peer_ffn(h1b, h1, u, vt, tabs, ln2_g[l].reshape(1, -1), ln2_b[l].reshape(1, -1), tb_ffn, eb)
    return h.reshape(batch, seq, D_MODEL)
```

```python
import functools
import math

import jax
import jax.numpy as jnp
from jax import lax
from jax.experimental import pallas as pl
from jax.experimental.pallas import tpu as pltpu

F32 = jnp.float32
BF16 = jnp.bfloat16
I32 = jnp.int32

D_MODEL = 1024
HEAD_DIM = 64
SWA_HEADS = 8
SWA_KV_HEADS = 2
WINDOW = 128
BLOCK = 128
DSA_HEADS = 8
IDX_HEADS = 4
IDX_DIM = 64
TOPK_MAX = 256
ROPE_THETA = 10000.0
PEER_HEADS = 8
PEER_NKEYS = 128
PEER_DKEY = 256
PEER_TOPK = 16
LN_EPS = 1e-5
DEPTH = 1
ALPHA = (2.0 * DEPTH) ** 0.25

LANES = 128
SUBLANES = 8
BF16_ROWS = 16
VMEM_LIMIT_BYTES = 56 * 1024 * 1024

MIX_COLS = 1732
MIX_COLS_PAD = 1792
WI_LANE0 = 64

NEG_BIG = -1e30
INT_MIN = -(2 ** 31)

_NT = (((1,), (1,)), ((), ()))


def _cparams(sem):
    return pltpu.CompilerParams(dimension_semantics=sem, vmem_limit_bytes=VMEM_LIMIT_BYTES)


def _layer_norm_rows(z, g, b):
    mu = jnp.mean(z, axis=-1, keepdims=True)
    zc = z - mu
    var = jnp.mean(zc * zc, axis=-1, keepdims=True)
    return zc * lax.rsqrt(var + LN_EPS) * g + b


def _in_proj_kernel(x_ref, w_ref, cos_ref, sin_ref, kg_ref, kb_ref,
                    qa_ref, ka_ref, va_ref, qb_ref, kb_out_ref, vb_ref, qi_ref, ki_ref, wi_ref):
    xb = x_ref[...].astype(BF16)
    proj = jnp.dot(xb, w_ref[...], preferred_element_type=F32)
    cos = cos_ref[...]
    sin = sin_ref[...]
    lane = lax.broadcasted_iota(I32, cos.shape, 1)
    first_half = (lane % HEAD_DIM) < (HEAD_DIM // 2)

    def rope(t):
        swapped = jnp.where(first_half, pltpu.roll(t, LANES - HEAD_DIM // 2, 1),
                            pltpu.roll(t, HEAD_DIM // 2, 1))
        return t * cos + swapped * sin

    def slab(i):
        return proj[:, i * LANES:(i + 1) * LANES]

    q_scale = HEAD_DIM ** -0.5
    for i in range(4):
        qa_ref[:, i * LANES:(i + 1) * LANES] = (rope(slab(i)) * q_scale).astype(BF16)
        qb_ref[:, i * LANES:(i + 1) * LANES] = (rope(slab(6 + i)) * q_scale).astype(BF16)
    ka_ref[...] = rope(slab(4)).astype(BF16)
    va_ref[...] = slab(5).astype(BF16)
    kv = slab(10)
    kv_r = rope(kv)
    kb_out_ref[...] = kv_r[:, :HEAD_DIM].astype(BF16)
    vb_ref[...] = jnp.where(lane < HEAD_DIM, pltpu.roll(kv, HEAD_DIM, 1),
                            jnp.where(lane == HEAD_DIM, 1.0, 0.0)).astype(BF16)
    for i in range(2):
        qi_ref[:, i * LANES:(i + 1) * LANES] = rope(slab(11 + i)).astype(BF16)
    last = slab(13)
    is_ki = lane < IDX_DIM
    mu = jnp.sum(jnp.where(is_ki, last, 0.0), axis=-1, keepdims=True) * (1.0 / IDX_DIM)
    cen = last - mu
    var = jnp.sum(jnp.where(is_ki, cen * cen, 0.0), axis=-1, keepdims=True) * (1.0 / IDX_DIM)
    kin = cen * lax.rsqrt(var + LN_EPS) * kg_ref[...] + kb_ref[...]
    kin = jnp.where(is_ki, kin, 0.0)
    ki_ref[...] = rope(kin)[:, :IDX_DIM].astype(BF16)
    w_scale = (IDX_HEADS ** -0.5) * (IDX_DIM ** -0.5)
    wi_ref[...] = last * w_scale


def _in_proj(x2, w_pad, cos_t, sin_t, kg, kb, seq, tm):
    n = x2.shape[0]
    nt = n // tm
    tps = seq // tm
    row = lambda w: pl.BlockSpec((tm, w), lambda i: (i, 0))
    out_shapes = [
        jax.ShapeDtypeStruct((n, 512), BF16), jax.ShapeDtypeStruct((n, 128), BF16),
        jax.ShapeDtypeStruct((n, 128), BF16), jax.ShapeDtypeStruct((n, 512), BF16),
        jax.ShapeDtypeStruct((n, 64), BF16), jax.ShapeDtypeStruct((n, 128), BF16),
        jax.ShapeDtypeStruct((n, 256), BF16), jax.ShapeDtypeStruct((n, 64), BF16),
        jax.ShapeDtypeStruct((n, 128), F32),
    ]
    return pl.pallas_call(
        _in_proj_kernel,
        grid=(nt,),
        in_specs=[
            row(D_MODEL),
            pl.BlockSpec((D_MODEL, MIX_COLS_PAD), lambda i: (0, 0)),
            pl.BlockSpec((tm, LANES), lambda i: (i % tps, 0)),
            pl.BlockSpec((tm, LANES), lambda i: (i % tps, 0)),
            pl.BlockSpec((1, LANES), lambda i: (0, 0)),
            pl.BlockSpec((1, LANES), lambda i: (0, 0)),
        ],
        out_specs=[row(512), row(128), row(128), row(512), row(64), row(128), row(256), row(64), row(128)],
        out_shape=out_shapes,
        compiler_params=_cparams(("parallel",)),
        name="in_proj",
    )(x2, w_pad, cos_t, sin_t, kg, kb)


def _swa_kernel(sink_ref, q_ref, kp_ref, kc_ref, vp_ref, vc_ref, o_ref):
    n = pl.program_id(1)
    qi = lax.broadcasted_iota(I32, (BLOCK, 2 * BLOCK), 0)
    sj = lax.broadcasted_iota(I32, (BLOCK, 2 * BLOCK), 1)
    diff = qi + BLOCK - sj
    valid = (diff >= 0) & (diff < WINDOW) & ((n > 0) | (sj >= BLOCK))
    group = SWA_HEADS // SWA_KV_HEADS
    for kh in range(SWA_KV_HEADS):
        cs = slice(kh * HEAD_DIM, (kh + 1) * HEAD_DIM)
        k = jnp.concatenate([kp_ref[:, cs], kc_ref[:, cs]], axis=0)
        v = jnp.concatenate([vp_ref[:, cs], vc_ref[:, cs]], axis=0)
        for g in range(group):
            h = kh * group + g
            hs = slice(h * HEAD_DIM, (h + 1) * HEAD_DIM)
            s = lax.dot_general(q_ref[:, hs], k, _NT, preferred_element_type=F32)
            s = jnp.where(valid, s, NEG_BIG)
            sink = sink_ref[h]
            m = jnp.maximum(jnp.max(s, axis=-1, keepdims=True), sink)
            p = jnp.exp(s - m)
            denom = jnp.sum(p, axis=-1, keepdims=True) + jnp.exp(sink - m)
            o = jnp.dot(p.astype(BF16), v, preferred_element_type=F32)
            o_ref[:, hs] = (o / denom).astype(o_ref.dtype)


def _swa(sinks, qa, ka, va, batch, seq):
    nb = seq // BLOCK
    cur = lambda w: pl.BlockSpec((BLOCK, w), lambda b, n: (b * nb + n, 0))
    prev = lambda w: pl.BlockSpec((BLOCK, w), lambda b, n: (b * nb + jnp.maximum(n - 1, 0), 0))
    return pl.pallas_call(
        _swa_kernel,
        grid=(batch, nb),
        in_specs=[pl.BlockSpec(memory_space=pltpu.SMEM), cur(512), prev(128), cur(128), prev(128), cur(128)],
        out_specs=cur(512),
        out_shape=jax.ShapeDtypeStruct((batch * seq, 512), BF16),
        compiler_params=_cparams(("parallel", "parallel")),
        name="swa",
    )(sinks, qa, ka, ka, va, va)


DSA_ROWS = 2 * BLOCK
DIGIT_BITS = 14


def _dsa_kernel(qb_ref, qi_ref, wi_ref, ki_ref, kb_ref, vb_ref, o_ref,
                keys_ref, keyt_ref, dig1_ref, dig2_ref, vst_ref, tie_ref, bias_ref, m_ref, acc_ref, qs_ref,
                s_ref, p_ref, alpha_ref,
                *, ck, nc, k_sel, seq_bits):
    n = pl.program_id(1)
    t0 = n * DSA_ROWS
    nck = (t0 + DSA_ROWS + ck - 1) // ck
    lane_tiles = ck // LANES
    qpos = t0 + lax.broadcasted_iota(I32, (DSA_ROWS, ck), 0)
    lane_pos = lax.broadcasted_iota(I32, (DSA_ROWS, ck), 1)

    qi = qi_ref[...]
    qi4 = jnp.concatenate([qi[:, h * IDX_DIM:(h + 1) * IDX_DIM] for h in range(IDX_HEADS)], axis=0)
    wi = wi_ref[...]
    wcols = [wi[:, WI_LANE0 + h:WI_LANE0 + h + 1] for h in range(IDX_HEADS)]

    def score_chunk(c, causal):
        r = lax.dot_general(qi4, ki_ref[c], _NT, preferred_element_type=F32)
        r = jnp.maximum(r, 0.0)
        isc = wcols[0] * r[0:DSA_ROWS]
        for h in range(1, IDX_HEADS):
            isc = isc + wcols[h] * r[h * DSA_ROWS:(h + 1) * DSA_ROWS]
        isc = jnp.where(isc == 0.0, 0.0, isc)
        bits = pltpu.bitcast(isc, I32)
        key = bits ^ ((bits >> 31) & 0x7FFFFFFF)
        if causal:
            key = jnp.where(c * ck + lane_pos <= qpos, key, INT_MIN)
        keys_ref[c] = key
        key_t = pltpu.bitcast(pltpu.bitcast(key, F32).T, I32)
        keyt_ref[c] = key_t
        key_u = key_t ^ INT_MIN
        dig1_ref[c] = pltpu.bitcast(digit_pattern(lax.shift_right_logical(key_u, 32 - DIGIT_BITS)), jnp.uint32)
        dig2_ref[c] = pltpu.bitcast(digit_pattern(lax.shift_right_logical(key_u, 32 - 2 * DIGIT_BITS)
                                                  & (2 ** DIGIT_BITS - 1)), jnp.uint32)

    def digit_pattern(d):
        return pltpu.bitcast(lax.shift_left(d + 128, 16), F32).astype(BF16)

    def scan_chunks(lo, hi, causal):
        def body(c, carry):
            score_chunk(c, causal)
            return carry
        lax.fori_loop(lo, hi, body, 0)

    n_past = (t0 + 1) // ck
    scan_chunks(0, n_past, False)
    scan_chunks(n_past, nck, True)

    sub_pos = lax.broadcasted_iota(I32, (SUBLANES, LANES), 0)
    groups = DSA_ROWS // LANES

    def to_rows(x):
        wide = jnp.broadcast_to(pltpu.bitcast(x[0:1, :], F32), (LANES, LANES))
        return pltpu.bitcast(wide.T, I32)

    def select(v):
        def count(g, pred):
            parts = [jnp.zeros((SUBLANES, LANES), I32) for _ in range(4)]
            i = 0
            for c in range(v):
                for kb in range(ck // SUBLANES):
                    tile = keyt_ref[c, kb * SUBLANES:(kb + 1) * SUBLANES, g * LANES:(g + 1) * LANES]
                    hit = pred(tile, c * ck + kb * SUBLANES + sub_pos)
                    parts[i % 4] = parts[i % 4] + jnp.where(hit, 1, 0)
                    i += 1
            tot = jnp.sum((parts[0] + parts[1]) + (parts[2] + parts[3]), axis=0, keepdims=True)
            return jnp.broadcast_to(tot, (SUBLANES, LANES))

        def bit_step(i, ts):
            bit = lax.shift_left(jnp.int32(1), 31 - i)
            out = []
            for g in range(groups):
                cand = ts[g] | bit
                cand_s = cand ^ INT_MIN
                out.append(jnp.where(count(g, lambda key, pos: key >= cand_s) >= k_sel, cand, ts[g]))
            return tuple(out)

        prow = SUBLANES * 4 // jnp.dtype(BF16).itemsize
        wrow = SUBLANES
        one16 = jnp.ones((), BF16)
        zero16 = jnp.zeros((), BF16)

        def pattern_rows(code):
            return digit_pattern(jnp.broadcast_to(code[0:1, :], (prow, LANES)))

        def count_packed(ref, g, pred):
            parts = [jnp.zeros((prow, LANES), BF16) for _ in range(4)]
            i = 0
            for c in range(v):
                for kb in range(ref.shape[1] // wrow):
                    tile = pltpu.bitcast(ref[c, kb * wrow:(kb + 1) * wrow, g * LANES:(g + 1) * LANES], BF16)
                    parts[i % 4] = parts[i % 4] + jnp.where(pred(tile), one16, zero16)
                    i += 1
            tot = jnp.sum(((parts[0] + parts[1]) + (parts[2] + parts[3])).astype(F32), axis=0, keepdims=True)
            return jnp.broadcast_to(tot, (SUBLANES, LANES))

        def digit_search(ref, base):
            def step(i, ds):
                bit = lax.shift_left(jnp.int32(1), DIGIT_BITS - 1 - i)
                out = []
                for g in range(groups):
                    cand = ds[g] | bit
                    pat = pattern_rows(cand)
                    cnt = base[g] + count_packed(ref, g, lambda t: t >= pat)
                    out.append(jnp.where(cnt >= k_sel, cand, ds[g]))
                return tuple(out)
            return lax.fori_loop(0, DIGIT_BITS, step, (zi,) * groups)

        zi = jnp.zeros((SUBLANES, LANES), I32)
        zf = jnp.zeros((SUBLANES, LANES), F32)
        d1 = digit_search(dig1_ref, (zf,) * groups)
        above, d2_base = [], []
        for g in range(groups):
            pat1 = pattern_rows(d1[g])
            d2_base.append(count_packed(dig1_ref, g, lambda t: t > pat1))
            for c in range(v):
                for kb in range(dig1_ref.shape[1] // wrow):
                    rows = slice(kb * wrow, (kb + 1) * wrow)
                    cols = slice(g * LANES, (g + 1) * LANES)
                    same = pltpu.bitcast(dig1_ref[c, rows, cols], BF16) == pat1
                    kept = jnp.where(same, pltpu.bitcast(dig2_ref[c, rows, cols], BF16), zero16)
                    dig2_ref[c, rows, cols] = pltpu.bitcast(kept, jnp.uint32)
        d2 = digit_search(dig2_ref, tuple(d2_base))
        ts = tuple(lax.shift_left(d1[g], 32 - DIGIT_BITS) | lax.shift_left(d2[g], 32 - 2 * DIGIT_BITS)
                   for g in range(groups))
        ts = lax.fori_loop(2 * DIGIT_BITS, 32, bit_step, ts)
        vstar = [t ^ INT_MIN for t in ts]
        need, surplus = [], []
        for g in range(groups):
            vs = vstar[g]
            vst_ref[g * LANES:(g + 1) * LANES, :] = to_rows(vs)
            tie_ref[g * LANES:(g + 1) * LANES, :] = jnp.full((LANES, LANES), 2 ** seq_bits, I32)
            cnt_gt = count(g, lambda key, pos: key > vs)
            cnt_eq = count(g, lambda key, pos: key == vs)
            need.append(k_sel - cnt_gt)
            surplus.append(jnp.max(cnt_eq - need[g]))

        pos_bits = (v * ck - 1).bit_length()
        for g in range(groups):
            @pl.when(surplus[g] > 0)
            def _(g=g):
                vs = vstar[g]
                cols = slice(g * LANES, (g + 1) * LANES)
                for c in range(v):
                    for kb in range(ck // SUBLANES):
                        rows = slice(kb * SUBLANES, (kb + 1) * SUBLANES)
                        pos = c * ck + kb * SUBLANES + sub_pos
                        keyt_ref[c, rows, cols] = jnp.where(keyt_ref[c, rows, cols] == vs, pos, 2 ** seq_bits)

                def idx_step(i, p):
                    cand = p | lax.shift_left(jnp.int32(1), pos_bits - 1 - i)
                    cnt = count(g, lambda tied_pos, pos: tied_pos < cand)
                    return jnp.where(cnt < need[g], cand, p)

                tie_ref[cols, :] = to_rows(lax.fori_loop(0, pos_bits, idx_step, zi))

    for v in range(1, nc + 1):
        pl.when(nck == v)(functools.partial(select, v))

    qb = qb_ref[...]
    for h in range(DSA_HEADS):
        qs_ref[h * DSA_ROWS:(h + 1) * DSA_ROWS, :] = qb[:, h * HEAD_DIM:(h + 1) * HEAD_DIM]
    m_ref[...] = jnp.full(m_ref.shape, NEG_BIG, F32)
    acc_ref[...] = jnp.zeros(acc_ref.shape, F32)
    row_lane = lax.broadcasted_iota(I32, (DSA_ROWS, LANES), 1)

    def stage_a(it, par):
        ca = jnp.minimum(it, nck - 1)
        s_ref[par] = lax.dot_general(qs_ref[...], kb_ref[ca], _NT, preferred_element_type=F32)

    def stage_b(it, par):
        cb = it - 1
        vst = vst_ref[...]
        tie = tie_ref[...]
        for j in range(lane_tiles):
            key = keys_ref[cb, :, j * LANES:(j + 1) * LANES]
            pos = cb * ck + j * LANES + row_lane
            sel = ((key > vst) | ((key == vst) & (pos <= tie))) & (key > INT_MIN)
            bias_ref[:, j * LANES:(j + 1) * LANES] = jnp.where(sel, 0.0, NEG_BIG)
        sb = 1 - par
        for h in range(DSA_HEADS):
            rows = slice(h * DSA_ROWS, (h + 1) * DSA_ROWS)
            s = s_ref[sb, rows, :] + bias_ref[...]
            s_ref[sb, rows, :] = s
            part = s[:, 0:LANES]
            for j in range(1, lane_tiles):
                part = jnp.maximum(part, s[:, j * LANES:(j + 1) * LANES])
            m_old = m_ref[h]
            m_new = jnp.maximum(m_old, jnp.max(part, axis=-1, keepdims=True))
            alpha_ref[sb, h] = jnp.exp(m_old - m_new)
            m_ref[h] = m_new
        for h in range(DSA_HEADS):
            rows = slice(h * DSA_ROWS, (h + 1) * DSA_ROWS)
            p_ref[sb, rows, :] = jnp.exp(s_ref[sb, rows, :] - m_ref[h]).astype(BF16)

    def stage_c(it, par):
        cc = jnp.maximum(it - 2, 0)
        pv = jnp.dot(p_ref[par], vb_ref[cc], preferred_element_type=F32)
        for h in range(DSA_HEADS):
            acc_ref[h] = alpha_ref[par, h] * acc_ref[h] + pv[h * DSA_ROWS:(h + 1) * DSA_ROWS]

    def attend_step(it, carry):
        for par in range(2):
            @pl.when(it % 2 == par)
            def _():
                stage_a(it, par)
                stage_b(it, par)
                stage_c(it, par)
        return carry

    p_ref[1] = jnp.zeros(p_ref.shape[1:], BF16)
    alpha_ref[1] = jnp.ones(alpha_ref.shape[1:], F32)
    stage_a(0, 0)
    lax.fori_loop(1, nck + 1, attend_step, 0)
    for par in range(2):
        pl.when((nck + 1) % 2 == par)(functools.partial(stage_c, nck + 1, par))
    for h in range(DSA_HEADS):
        a = acc_ref[h]
        o_ref[:, h * HEAD_DIM:(h + 1) * HEAD_DIM] = (a[:, :HEAD_DIM] / a[:, HEAD_DIM:HEAD_DIM + 1]).astype(o_ref.dtype)


def _dsa(qb, qi, wi, ki, kb, vb, batch, seq, ck):
    nq = seq // DSA_ROWS
    nc = seq // ck
    k_sel = min(TOPK_MAX, seq // 4)
    seq_bits = int(math.log2(seq))
    assert 2 ** seq_bits == seq and k_sel <= ck and seq % DSA_ROWS == 0
    assert seq // BF16_ROWS <= 256
    chunked = lambda a: a.reshape(batch, nc, ck, a.shape[-1])
    blk = lambda w: pl.BlockSpec((DSA_ROWS, w), lambda b, n: (b * nq + n, 0))
    whole = lambda w: pl.BlockSpec((None, nc, ck, w), lambda b, n: (b, 0, 0, 0))
    kernel = functools.partial(_dsa_kernel, ck=ck, nc=nc, k_sel=k_sel, seq_bits=seq_bits)
    return pl.pallas_call(
        kernel,
        grid=(batch, nq),
        in_specs=[blk(512), blk(256), blk(128), whole(IDX_DIM), whole(HEAD_DIM), whole(LANES)],
        out_specs=blk(512),
        out_shape=jax.ShapeDtypeStruct((batch * seq, 512), BF16),
        scratch_shapes=[
            pltpu.VMEM((nc, DSA_ROWS, ck), I32),
            pltpu.VMEM((nc, ck, DSA_ROWS), I32),
            pltpu.VMEM((nc, ck * jnp.dtype(BF16).itemsize // 4, DSA_ROWS), jnp.uint32),
            pltpu.VMEM((nc, ck * jnp.dtype(BF16).itemsize // 4, DSA_ROWS), jnp.uint32),
            pltpu.VMEM((DSA_ROWS, LANES), I32),
            pltpu.VMEM((DSA_ROWS, LANES), I32),
            pltpu.VMEM((DSA_ROWS, ck), F32),
            pltpu.VMEM((DSA_HEADS, DSA_ROWS, 1), F32),
            pltpu.VMEM((DSA_HEADS, DSA_ROWS, LANES), F32),
            pltpu.VMEM((DSA_HEADS * DSA_ROWS, HEAD_DIM), BF16),
            pltpu.VMEM((2, DSA_HEADS * DSA_ROWS, ck), F32),
            pltpu.VMEM((2, DSA_HEADS * DSA_ROWS, ck), BF16),
            pltpu.VMEM((2, DSA_HEADS, DSA_ROWS, 1), F32),
        ],
        compiler_params=_cparams(("parallel", "arbitrary")),
        name="dsa",
    )(qb, qi, wi, chunked(ki), chunked(kb), chunked(vb))


def _out_proj_kernel(oa_ref, ob_ref, x_ref, wa_ref, wb_ref, g_ref, b_ref, h_ref, hb_ref):
    mix = jnp.dot(oa_ref[...], wa_ref[...], preferred_element_type=F32)
    mix = mix + jnp.dot(ob_ref[...], wb_ref[...], preferred_element_type=F32)
    h = _layer_norm_rows(ALPHA * x_ref[...] + mix, g_ref[...], b_ref[...])
    h_ref[...] = h
    hb_ref[...] = h.astype(BF16)


def _out_proj(oa, ob, x2, wo_a, wo_b, g, b, tm):
    n = x2.shape[0]
    row = lambda w: pl.BlockSpec((tm, w), lambda i: (i, 0))
    const = lambda r, c: pl.BlockSpec((r, c), lambda i: (0, 0))
    return pl.pallas_call(
        _out_proj_kernel,
        grid=(n // tm,),
        in_specs=[row(512), row(512), row(D_MODEL), const(512, D_MODEL), const(512, D_MODEL),
                  const(1, D_MODEL), const(1, D_MODEL)],
        out_specs=[row(D_MODEL), row(D_MODEL)],
        out_shape=[jax.ShapeDtypeStruct((n, D_MODEL), F32), jax.ShapeDtypeStruct((n, D_MODEL), BF16)],
        compiler_params=_cparams(("parallel",)),
        name="out_proj",
    )(oa, ob, x2, wo_a, wo_b, g, b)


def _top_values(s, count, with_rank):
    vals = []
    rank = jnp.full(s.shape, float(count), F32) if with_rank else None
    cur = s
    for j in range(count):
        m = jnp.max(cur, axis=0, keepdims=True)
        vals.append(m)
        below = cur < m
        cur = jnp.where(below, cur, -jnp.inf)
        if with_rank:
            rank = jnp.where(below, rank, jnp.minimum(rank, float(j)))
    return vals, rank


def _oddeven_merge_sort_pairs(n):
    pairs = []

    def merge(lo, hi, r):
        step = r * 2
        if step < hi - lo:
            merge(lo, hi, step)
            merge(lo + r, hi, step)
            pairs.extend((i, i + r) for i in range(lo + r, hi - r, step))
        else:
            pairs.append((lo, lo + r))

    def sort(lo, hi):
        if hi - lo >= 1:
            mid = lo + (hi - lo) // 2
            sort(lo, mid)
            sort(mid + 1, hi)
            merge(lo, hi, 1)

    sort(0, n - 1)
    return pairs


def _sorted_top16(s):
    n = PEER_NKEYS // SUBLANES
    x = [s[i * SUBLANES:(i + 1) * SUBLANES, :] for i in range(n)]
    for i, j in _oddeven_merge_sort_pairs(n):
        x[i], x[j] = jnp.maximum(x[i], x[j]), jnp.minimum(x[i], x[j])
    for shift in (4, 2, 1):
        y = [jnp.maximum(x[i], pltpu.roll(x[n - 1 - i], shift, 0)) for i in range(n)]
        for stride in (8, 4, 2, 1):
            for i in range(n):
                if i & stride == 0:
                    y[i], y[i + stride] = jnp.maximum(y[i], y[i + stride]), jnp.minimum(y[i], y[i + stride])
        x = y
    return x


def _peer_route_kernel(h_ref, wqt_ref, sk_ref, r2_ref, p2_ref, na_ref, p1_ref):
    qt = lax.dot_general(wqt_ref[...], h_ref[...], _NT, preferred_element_type=F32)
    half = PEER_DKEY // 2
    for h in range(PEER_HEADS):
        s = []
        for p in range(2):
            idx = h * 2 + p
            qhp = qt[idx * half:(idx + 1) * half, :].astype(BF16)
            s.append(jnp.dot(sk_ref[idx], qhp, preferred_element_type=F32))
        s1, s2 = s
        v1 = [x[0:1] for x in _sorted_top16(s1)]
        top2 = _sorted_top16(s2)
        v2 = [x[0:1] for x in top2]
        rank2 = []
        for i in range(PEER_NKEYS // SUBLANES):
            blk = s2[i * SUBLANES:(i + 1) * SUBLANES, :]
            r = jnp.where(blk < top2[0], 1.0, 0.0)
            for j in range(1, PEER_TOPK):
                r = r + jnp.where(blk < top2[j], 1.0, 0.0)
            rank2.append(r)
        rank2 = jnp.concatenate(rank2, axis=0)
        v2_all = jnp.concatenate(v2, axis=0)
        sub = lax.broadcasted_iota(I32, (8, v2_all.shape[1]), 0)
        cands = [v1[0] + v2_all]
        for i in range(1, 8):
            lim = PEER_TOPK // (i + 1)
            cands.append(jnp.where(sub < lim, v1[i] + v2_all[:8], -jnp.inf))
        cands.append(jnp.concatenate(v1[8:], axis=0) + v2[0])
        cand = jnp.concatenate(cands, axis=0)
        tops, _ = _top_values(cand, PEER_TOPK, False)
        tau = tops[-1]
        tau = jnp.where(tau > -jnp.inf, tau, jnp.min(jnp.where(cand > -jnp.inf, cand, jnp.inf), axis=0,
                                                     keepdims=True))
        mx = v1[0] + v2[0]
        z = jnp.sum(jnp.where(cand >= tau, jnp.exp(cand - mx), 0.0), axis=0, keepdims=True)
        na = jnp.zeros(s1.shape, F32)
        for j in range(8):
            na = na + jnp.where((s1 + v2[j]) >= tau, 1.0, 0.0)
        n_top = jnp.sum(jnp.where((v1[0] + v2_all[8:]) >= tau, 1.0, 0.0), axis=0, keepdims=True)
        na = na + jnp.where(s1 == v1[0], n_top, 0.0)
        r2_ref[h] = pltpu.bitcast(rank2.astype(BF16), jnp.uint32)
        p2_ref[h] = pltpu.bitcast(jnp.exp(s2 - v2[0]).astype(BF16), jnp.uint32)
        na_ref[h] = na
        p1_ref[h] = jnp.exp(s1 - v1[0]) / z


def _peer_route(hb, wqt, sk, tb):
    n = hb.shape[0]
    word_rows = PEER_NKEYS * jnp.dtype(BF16).itemsize // 4
    tab16 = jax.ShapeDtypeStruct((PEER_HEADS, word_rows, n), jnp.uint32)
    tab32 = jax.ShapeDtypeStruct((PEER_HEADS, PEER_NKEYS, n), F32)
    spec16 = pl.BlockSpec((PEER_HEADS, word_rows, tb), lambda i: (0, 0, i))
    spec32 = pl.BlockSpec((PEER_HEADS, PEER_NKEYS, tb), lambda i: (0, 0, i))
    return pl.pallas_call(
        _peer_route_kernel,
        grid=(n // tb,),
        in_specs=[pl.BlockSpec((tb, D_MODEL), lambda i: (i, 0)),
                  pl.BlockSpec(wqt.shape, lambda i: (0, 0)),
                  pl.BlockSpec(sk.shape, lambda i: (0, 0, 0))],
        out_specs=[spec16, spec16, spec32, spec32],
        out_shape=[tab16, tab16, tab32, tab32],
        compiler_params=_cparams(("parallel",)),
        name="peer_route",
    )(hb, wqt, sk)


def _peer_ffn_kernel(hb_ref, h_ref, u_ref, vt_ref, r2_ref, p2_ref, na_ref, p1_ref, g_ref, b_ref,
                     o_ref, acc_ref, w_ref, *, tb, eb, sub):
    j = pl.program_id(1)

    @pl.when(j == 0)
    def _():
        acc_ref[...] = jnp.zeros(acc_ref.shape, F32)

    groups = PEER_NKEYS // BF16_ROWS
    wr = r2_ref.shape[1] // groups
    zero = jnp.zeros((), BF16)
    for sb in range(eb // sub):
        srows = slice(sb * sub, (sb + 1) * sub)
        act_t = lax.dot_general(u_ref[srows, :], hb_ref[...], _NT, preferred_element_type=F32)
        for asub in range(sub // PEER_NKEYS):
            al = sb * (sub // PEER_NKEYS) + asub
            for tt in range(tb // LANES):
                cols = slice(tt * LANES, (tt + 1) * LANES)
                gate = [jnp.zeros((BF16_ROWS, LANES), BF16) for _ in range(groups)]
                for h in range(PEER_HEADS):
                    na = jnp.broadcast_to(na_ref[h, al:al + 1, cols], (BF16_ROWS, LANES)).astype(BF16)
                    p1 = jnp.broadcast_to(p1_ref[h, al:al + 1, cols], (BF16_ROWS, LANES)).astype(BF16)
                    for g in range(groups):
                        ws = slice(g * wr, (g + 1) * wr)
                        r2 = pltpu.bitcast(r2_ref[h, ws, cols], BF16)
                        p2 = pltpu.bitcast(p2_ref[h, ws, cols], BF16)
                        gate[g] = gate[g] + jnp.where(r2 < na, p2, zero) * p1
                for g in range(groups):
                    lrows = slice(asub * PEER_NKEYS + g * BF16_ROWS, asub * PEER_NKEYS + (g + 1) * BF16_ROWS)
                    x = act_t[lrows, cols]
                    gelu = 0.5 * x * (1.0 + lax.erf(x * (2.0 ** -0.5)))
                    w_ref[sb, lrows, cols] = gelu.astype(BF16) * gate[g]
    w_all = w_ref[...].reshape(eb, tb)
    acc_ref[...] += jnp.dot(vt_ref[...], w_all, preferred_element_type=F32)

    @pl.when(j == pl.num_programs(1) - 1)
    def _():
        z = ALPHA * h_ref[...] + acc_ref[...].T
        o_ref[...] = _layer_norm_rows(z, g_ref[...], b_ref[...])


def _peer_ffn(hb, h, u, vt, tabs, g, b, tb, eb):
    n = hb.shape[0]
    ne = u.shape[0]
    tab_spec = pl.BlockSpec((PEER_HEADS, tabs[0].shape[1], tb), lambda i, j: (0, 0, i))
    a_spec = pl.BlockSpec((PEER_HEADS, eb // PEER_NKEYS, tb), lambda i, j: (0, j, i))
    sub = 512
    kernel = functools.partial(_peer_ffn_kernel, tb=tb, eb=eb, sub=sub)
    return pl.pallas_call(
        kernel,
        grid=(n // tb, ne // eb),
        in_specs=[pl.BlockSpec((tb, D_MODEL), lambda i, j: (i, 0)),
                  pl.BlockSpec((tb, D_MODEL), lambda i, j: (i, 0)),
                  pl.BlockSpec((eb, D_MODEL), lambda i, j: (j, 0)),
                  pl.BlockSpec((D_MODEL, eb), lambda i, j: (0, j)),
                  tab_spec, tab_spec, a_spec, a_spec,
                  pl.BlockSpec((1, D_MODEL), lambda i, j: (0, 0)),
                  pl.BlockSpec((1, D_MODEL), lambda i, j: (0, 0))],
        out_specs=pl.BlockSpec((tb, D_MODEL), lambda i, j: (i, 0)),
        out_shape=jax.ShapeDtypeStruct((n, D_MODEL), F32),
        scratch_shapes=[pltpu.VMEM((D_MODEL, tb), F32), pltpu.VMEM((eb // sub, sub, tb), BF16)],
        compiler_params=_cparams(("parallel", "arbitrary")),
        name="peer_ffn",
    )(hb, h, u, vt, *tabs, g, b)


def _tiles(n_tokens, seq):
    tm = 256 if seq % 256 == 0 else BLOCK
    ck = 512 if seq >= 2048 else BLOCK
    tb_route = 256 if n_tokens % 256 == 0 else BLOCK
    tb_ffn = 512 if n_tokens % 512 == 0 else BLOCK
    eb = 2048
    return tm, ck, tb_route, tb_ffn, eb


def kernel(x, w_in, idx_k_g, idx_k_b, sinks, w_o, ln1_g, ln1_b, peer_wq, peer_subkeys, peer_u, peer_v,
           ln2_g, ln2_b):
    batch, seq, _ = x.shape
    n = batch * seq
    tm, ck, tb_route, tb_ffn, eb = _tiles(n, seq)

    inv = ROPE_THETA ** (-jnp.arange(0, HEAD_DIM, 2, dtype=F32) / HEAD_DIM)
    ang = jnp.arange(seq, dtype=F32)[:, None] * inv[None, :]
    cos, sin = jnp.cos(ang), jnp.sin(ang)
    cos_t = jnp.concatenate([cos, cos, cos, cos], axis=-1)
    sin_t = jnp.concatenate([-sin, sin, -sin, sin], axis=-1)

    h = x.reshape(n, D_MODEL)
    for l in range(DEPTH):
        w_pad = jnp.pad(w_in[l], ((0, 0), (0, MIX_COLS_PAD - MIX_COLS))).astype(BF16)
        kg = jnp.pad(idx_k_g[l], (0, LANES - IDX_DIM)).reshape(1, LANES)
        kb = jnp.pad(idx_k_b[l], (0, LANES - IDX_DIM)).reshape(1, LANES)
        qa, ka, va, qb, kbb, vb, qi, ki, wi = _in_proj(h, w_pad, cos_t, sin_t, kg, kb, seq, tm)
        oa = _swa(sinks[l], qa, ka, va, batch, seq)
        ob = _dsa(qb, qi, wi, ki, kbb, vb, batch, seq, ck)
        wo = w_o[l].astype(BF16)
        h1, h1b = _out_proj(oa, ob, h, wo[:512], wo[512:], ln1_g[l].reshape(1, -1), ln1_b[l].reshape(1, -1), tm)
        wqt = peer_wq[l].T.astype(BF16)
        sk = peer_subkeys[l].reshape(PEER_HEADS * 2, PEER_NKEYS, PEER_DKEY // 2).astype(BF16)
        tabs = _peer_route(h1b, wqt, sk, tb_route)
        u = peer_u[l].astype(BF16)
        vt = peer_v[l].T.astype(BF16)
        h = _peer_ffn(h1b, h1, u, vt, tabs, ln2_g[l].reshape(1, -1), ln2_b[l].reshape(1, -1), tb_ffn, eb)
    return h.reshape(batch, seq, D_MODEL)
```

```python
import functools
import math

import jax
import jax.numpy as jnp
from jax import lax
from jax.experimental import pallas as pl
from jax.experimental.pallas import tpu as pltpu

F32 = jnp.float32
BF16 = jnp.bfloat16
I32 = jnp.int32

D_MODEL = 1024
HEAD_DIM = 64
SWA_HEADS = 8
SWA_KV_HEADS = 2
WINDOW = 128
BLOCK = 128
DSA_HEADS = 8
IDX_HEADS = 4
IDX_DIM = 64
TOPK_MAX = 256
ROPE_THETA = 10000.0
PEER_HEADS = 8
PEER_NKEYS = 128
PEER_DKEY = 256
PEER_TOPK = 16
LN_EPS = 1e-5
DEPTH = 1
ALPHA = (2.0 * DEPTH) ** 0.25

LANES = 128
SUBLANES = 8
BF16_ROWS = 16
VMEM_LIMIT_BYTES = 56 * 1024 * 1024

MIX_COLS = 1732
MIX_COLS_PAD = 1792
WI_LANE0 = 64

NEG_BIG = -1e30
INT_MIN = -(2 ** 31)

_NT = (((1,), (1,)), ((), ()))


def _cparams(sem):
    return pltpu.CompilerParams(dimension_semantics=sem, vmem_limit_bytes=VMEM_LIMIT_BYTES)


def _layer_norm_rows(z, g, b):
    mu = jnp.mean(z, axis=-1, keepdims=True)
    zc = z - mu
    var = jnp.mean(zc * zc, axis=-1, keepdims=True)
    return zc * lax.rsqrt(var + LN_EPS) * g + b


def _in_proj_kernel(x_ref, w_ref, cos_ref, sin_ref, kg_ref, kb_ref,
                    qa_ref, ka_ref, va_ref, qb_ref, kb_out_ref, vb_ref, qi_ref, ki_ref, wi_ref):
    xb = x_ref[...].astype(BF16)
    proj = jnp.dot(xb, w_ref[...], preferred_element_type=F32)
    cos = cos_ref[...]
    sin = sin_ref[...]
    lane = lax.broadcasted_iota(I32, cos.shape, 1)
    first_half = (lane % HEAD_DIM) < (HEAD_DIM // 2)

    def rope(t):
        swapped = jnp.where(first_half, pltpu.roll(t, LANES - HEAD_DIM // 2, 1),
                            pltpu.roll(t, HEAD_DIM // 2, 1))
        return t * cos + swapped * sin

    def slab(i):
        return proj[:, i * LANES:(i + 1) * LANES]

    q_scale = HEAD_DIM ** -0.5
    for i in range(4):
        qa_ref[:, i * LANES:(i + 1) * LANES] = (rope(slab(i)) * q_scale).astype(BF16)
        qb_ref[:, i * LANES:(i + 1) * LANES] = (rope(slab(6 + i)) * q_scale).astype(BF16)
    ka_ref[...] = rope(slab(4)).astype(BF16)
    va_ref[...] = slab(5).astype(BF16)
    kv = slab(10)
    kv_r = rope(kv)
    kb_out_ref[...] = kv_r[:, :HEAD_DIM].astype(BF16)
    vb_ref[...] = jnp.where(lane < HEAD_DIM, pltpu.roll(kv, HEAD_DIM, 1),
                            jnp.where(lane == HEAD_DIM, 1.0, 0.0)).astype(BF16)
    for i in range(2):
        qi_ref[:, i * LANES:(i + 1) * LANES] = rope(slab(11 + i)).astype(BF16)
    last = slab(13)
    is_ki = lane < IDX_DIM
    mu = jnp.sum(jnp.where(is_ki, last, 0.0), axis=-1, keepdims=True) * (1.0 / IDX_DIM)
    cen = last - mu
    var = jnp.sum(jnp.where(is_ki, cen * cen, 0.0), axis=-1, keepdims=True) * (1.0 / IDX_DIM)
    kin = cen * lax.rsqrt(var + LN_EPS) * kg_ref[...] + kb_ref[...]
    kin = jnp.where(is_ki, kin, 0.0)
    ki_ref[...] = rope(kin)[:, :IDX_DIM].astype(BF16)
    w_scale = (IDX_HEADS ** -0.5) * (IDX_DIM ** -0.5)
    wi_ref[...] = last * w_scale


def _in_proj(x2, w_pad, cos_t, sin_t, kg, kb, seq, tm):
    n = x2.shape[0]
    nt = n // tm
    tps = seq // tm
    row = lambda w: pl.BlockSpec((tm, w), lambda i: (i, 0))
    out_shapes = [
        jax.ShapeDtypeStruct((n, 512), BF16), jax.ShapeDtypeStruct((n, 128), BF16),
        jax.ShapeDtypeStruct((n, 128), BF16), jax.ShapeDtypeStruct((n, 512), BF16),
        jax.ShapeDtypeStruct((n, 64), BF16), jax.ShapeDtypeStruct((n, 128), BF16),
        jax.ShapeDtypeStruct((n, 256), BF16), jax.ShapeDtypeStruct((n, 64), BF16),
        jax.ShapeDtypeStruct((n, 128), F32),
    ]
    return pl.pallas_call(
        _in_proj_kernel,
        grid=(nt,),
        in_specs=[
            row(D_MODEL),
            pl.BlockSpec((D_MODEL, MIX_COLS_PAD), lambda i: (0, 0)),
            pl.BlockSpec((tm, LANES), lambda i: (i % tps, 0)),
            pl.BlockSpec((tm, LANES), lambda i: (i % tps, 0)),
            pl.BlockSpec((1, LANES), lambda i: (0, 0)),
            pl.BlockSpec((1, LANES), lambda i: (0, 0)),
        ],
        out_specs=[row(512), row(128), row(128), row(512), row(64), row(128), row(256), row(64), row(128)],
        out_shape=out_shapes,
        compiler_params=_cparams(("parallel",)),
        name="in_proj",
    )(x2, w_pad, cos_t, sin_t, kg, kb)


def _swa_kernel(sink_ref, q_ref, kp_ref, kc_ref, vp_ref, vc_ref, o_ref):
    n = pl.program_id(1)
    qi = lax.broadcasted_iota(I32, (BLOCK, 2 * BLOCK), 0)
    sj = lax.broadcasted_iota(I32, (BLOCK, 2 * BLOCK), 1)
    diff = qi + BLOCK - sj
    valid = (diff >= 0) & (diff < WINDOW) & ((n > 0) | (sj >= BLOCK))
    group = SWA_HEADS // SWA_KV_HEADS
    for kh in range(SWA_KV_HEADS):
        cs = slice(kh * HEAD_DIM, (kh + 1) * HEAD_DIM)
        k = jnp.concatenate([kp_ref[:, cs], kc_ref[:, cs]], axis=0)
        v = jnp.concatenate([vp_ref[:, cs], vc_ref[:, cs]], axis=0)
        for g in range(group):
            h = kh * group + g
            hs = slice(h * HEAD_DIM, (h + 1) * HEAD_DIM)
            s = lax.dot_general(q_ref[:, hs], k, _NT, preferred_element_type=F32)
            s = jnp.where(valid, s, NEG_BIG)
            sink = sink_ref[h]
            m = jnp.maximum(jnp.max(s, axis=-1, keepdims=True), sink)
            p = jnp.exp(s - m)
            denom = jnp.sum(p, axis=-1, keepdims=True) + jnp.exp(sink - m)
            o = jnp.dot(p.astype(BF16), v, preferred_element_type=F32)
            o_ref[:, hs] = (o / denom).astype(o_ref.dtype)


def _swa(sinks, qa, ka, va, batch, seq):
    nb = seq // BLOCK
    cur = lambda w: pl.BlockSpec((BLOCK, w), lambda b, n: (b * nb + n, 0))
    prev = lambda w: pl.BlockSpec((BLOCK, w), lambda b, n: (b * nb + jnp.maximum(n - 1, 0), 0))
    return pl.pallas_call(
        _swa_kernel,
        grid=(batch, nb),
        in_specs=[pl.BlockSpec(memory_space=pltpu.SMEM), cur(512), prev(128), cur(128), prev(128), cur(128)],
        out_specs=cur(512),
        out_shape=jax.ShapeDtypeStruct((batch * seq, 512), BF16),
        compiler_params=_cparams(("parallel", "parallel")),
        name="swa",
    )(sinks, qa, ka, ka, va, va)


DSA_ROWS = 2 * BLOCK
DIGIT_BITS = 14


def _dsa_kernel(qb_ref, qi_ref, wi_ref, ki_ref, kb_ref, vb_ref, o_ref,
                keys_ref, keyt_ref, dig1_ref, dig2_ref, vst_ref, tie_ref, bias_ref, m_ref, acc_ref, qs_ref,
                s_ref, p_ref, alpha_ref,
                *, ck, nc, k_sel, seq_bits):
    n = pl.program_id(1)
    t0 = n * DSA_ROWS
    nck = (t0 + DSA_ROWS + ck - 1) // ck
    lane_tiles = ck // LANES
    qpos = t0 + lax.broadcasted_iota(I32, (DSA_ROWS, ck), 0)
    lane_pos = lax.broadcasted_iota(I32, (DSA_ROWS, ck), 1)

    qi = qi_ref[...]
    qi4 = jnp.concatenate([qi[:, h * IDX_DIM:(h + 1) * IDX_DIM] for h in range(IDX_HEADS)], axis=0)
    wi = wi_ref[...]
    wcols = [wi[:, WI_LANE0 + h:WI_LANE0 + h + 1] for h in range(IDX_HEADS)]

    def score_chunk(c, causal):
        r = lax.dot_general(qi4, ki_ref[c], _NT, preferred_element_type=F32)
        r = jnp.maximum(r, 0.0)
        isc = wcols[0] * r[0:DSA_ROWS]
        for h in range(1, IDX_HEADS):
            isc = isc + wcols[h] * r[h * DSA_ROWS:(h + 1) * DSA_ROWS]
        isc = jnp.where(isc == 0.0, 0.0, isc)
        bits = pltpu.bitcast(isc, I32)
        key = bits ^ ((bits >> 31) & 0x7FFFFFFF)
        if causal:
            key = jnp.where(c * ck + lane_pos <= qpos, key, INT_MIN)
        keys_ref[c] = key
        key_t = pltpu.bitcast(pltpu.bitcast(key, F32).T, I32)
        keyt_ref[c] = key_t
        key_u = key_t ^ INT_MIN
        dig1_ref[c] = pltpu.bitcast(digit_pattern(lax.shift_right_logical(key_u, 32 - DIGIT_BITS)), jnp.uint32)
        dig2_ref[c] = pltpu.bitcast(digit_pattern(lax.shift_right_logical(key_u, 32 - 2 * DIGIT_BITS)
                                                  & (2 ** DIGIT_BITS - 1)), jnp.uint32)

    def digit_pattern(d):
        return pltpu.bitcast(lax.shift_left(d + 128, 16), F32).astype(BF16)

    def scan_chunks(lo, hi, causal):
        def body(c, carry):
            score_chunk(c, causal)
            return carry
        lax.fori_loop(lo, hi, body, 0)

    n_past = (t0 + 1) // ck
    scan_chunks(0, n_past, False)
    scan_chunks(n_past, nck, True)

    sub_pos = lax.broadcasted_iota(I32, (SUBLANES, LANES), 0)
    groups = DSA_ROWS // LANES

    def to_rows(x):
        wide = jnp.broadcast_to(pltpu.bitcast(x[0:1, :], F32), (LANES, LANES))
        return pltpu.bitcast(wide.T, I32)

    def select(v):
        def count(g, pred):
            parts = [jnp.zeros((SUBLANES, LANES), I32) for _ in range(4)]
            i = 0
            for c in range(v):
                for kb in range(ck // SUBLANES):
                    tile = keyt_ref[c, kb * SUBLANES:(kb + 1) * SUBLANES, g * LANES:(g + 1) * LANES]
                    hit = pred(tile, c * ck + kb * SUBLANES + sub_pos)
                    parts[i % 4] = parts[i % 4] + jnp.where(hit, 1, 0)
                    i += 1
            tot = jnp.sum((parts[0] + parts[1]) + (parts[2] + parts[3]), axis=0, keepdims=True)
            return jnp.broadcast_to(tot, (SUBLANES, LANES))

        def bit_step(i, ts):
            bit = lax.shift_left(jnp.int32(1), 31 - i)
            out = []
            for g in range(groups):
                cand = ts[g] | bit
                cand_s = cand ^ INT_MIN
                out.append(jnp.where(count(g, lambda key, pos: key >= cand_s) >= k_sel, cand, ts[g]))
            return tuple(out)

        prow = SUBLANES * 4 // jnp.dtype(BF16).itemsize
        wrow = SUBLANES
        one16 = jnp.ones((), BF16)
        zero16 = jnp.zeros((), BF16)

        def pattern_rows(code):
            return digit_pattern(jnp.broadcast_to(code[0:1, :], (prow, LANES)))

        def count_packed(ref, g, pred):
            parts = [jnp.zeros((prow, LANES), BF16) for _ in range(4)]
            i = 0
            for c in range(v):
                for kb in range(ref.shape[1] // wrow):
                    tile = pltpu.bitcast(ref[c, kb * wrow:(kb + 1) * wrow, g * LANES:(g + 1) * LANES], BF16)
                    parts[i % 4] = parts[i % 4] + jnp.where(pred(tile), one16, zero16)
                    i += 1
            tot = jnp.sum(((parts[0] + parts[1]) + (parts[2] + parts[3])).astype(F32), axis=0, keepdims=True)
            return jnp.broadcast_to(tot, (SUBLANES, LANES))

        def digit_search(ref, base):
            def step(i, ds):
                bit = lax.shift_left(jnp.int32(1), DIGIT_BITS - 1 - i)
                out = []
                for g in range(groups):
                    cand = ds[g] | bit
                    pat = pattern_rows(cand)
                    cnt = base[g] + count_packed(ref, g, lambda t: t >= pat)
                    out.append(jnp.where(cnt >= k_sel, cand, ds[g]))
                return tuple(out)
            return lax.fori_loop(0, DIGIT_BITS, step, (zi,) * groups)

        zi = jnp.zeros((SUBLANES, LANES), I32)
        zf = jnp.zeros((SUBLANES, LANES), F32)
        d1 = digit_search(dig1_ref, (zf,) * groups)
        above, d2_base = [], []
        for g in range(groups):
            pat1 = pattern_rows(d1[g])
            d2_base.append(count_packed(dig1_ref, g, lambda t: t > pat1))
            for c in range(v):
                for kb in range(dig1_ref.shape[1] // wrow):
                    rows = slice(kb * wrow, (kb + 1) * wrow)
                    cols = slice(g * LANES, (g + 1) * LANES)
                    same = pltpu.bitcast(dig1_ref[c, rows, cols], BF16) == pat1
                    kept = jnp.where(same, pltpu.bitcast(dig2_ref[c, rows, cols], BF16), zero16)
                    dig2_ref[c, rows, cols] = pltpu.bitcast(kept, jnp.uint32)
        d2 = digit_search(dig2_ref, tuple(d2_base))
        ts = tuple(lax.shift_left(d1[g], 32 - DIGIT_BITS) | lax.shift_left(d2[g], 32 - 2 * DIGIT_BITS)
                   for g in range(groups))
        ts = lax.fori_loop(2 * DIGIT_BITS, 32, bit_step, ts)
        vstar = [t ^ INT_MIN for t in ts]
        need, surplus = [], []
        for g in range(groups):
            vs = vstar[g]
            vst_ref[g * LANES:(g + 1) * LANES, :] = to_rows(vs)
            tie_ref[g * LANES:(g + 1) * LANES, :] = jnp.full((LANES, LANES), 2 ** seq_bits, I32)
            cnt_gt = count(g, lambda key, pos: key > vs)
            cnt_eq = count(g, lambda key, pos: key == vs)
            need.append(k_sel - cnt_gt)
            surplus.append(jnp.max(cnt_eq - need[g]))

        pos_bits = (v * ck - 1).bit_length()
        for g in range(groups):
            @pl.when(surplus[g] > 0)
            def _(g=g):
                cols = slice(g * LANES, (g + 1) * LANES)
                vs = jnp.broadcast_to(vstar[g][0:1, :], (prow, LANES))
                pack_pos = lax.broadcasted_iota(I32, (prow, LANES), 0)
                for c in range(v):
                    for kb in range(ck // prow):
                        key = keyt_ref[c, kb * prow:(kb + 1) * prow, cols]
                        tied_pos = jnp.where(key == vs, c * ck + kb * prow + pack_pos, 2 ** seq_bits)
                        dig2_ref[c, kb * wrow:(kb + 1) * wrow, cols] = pltpu.bitcast(digit_pattern(tied_pos),
                                                                                     jnp.uint32)
                need_f = need[g].astype(F32)

                def idx_step(i, p):
                    cand = p | lax.shift_left(jnp.int32(1), pos_bits - 1 - i)
                    pat = pattern_rows(cand)
                    cnt = count_packed(dig2_ref, g, lambda t: t < pat)
                    return jnp.where(cnt < need_f, cand, p)

                tie_ref[cols, :] = to_rows(lax.fori_loop(0, pos_bits, idx_step, zi))

    for v in range(1, nc + 1):
        pl.when(nck == v)(functools.partial(select, v))

    qb = qb_ref[...]
    for h in range(DSA_HEADS):
        qs_ref[h * DSA_ROWS:(h + 1) * DSA_ROWS, :] = qb[:, h * HEAD_DIM:(h + 1) * HEAD_DIM]
    m_ref[...] = jnp.full(m_ref.shape, NEG_BIG, F32)
    acc_ref[...] = jnp.zeros(acc_ref.shape, F32)
    row_lane = lax.broadcasted_iota(I32, (DSA_ROWS, LANES), 1)

    def stage_a(it, par):
        ca = jnp.minimum(it, nck - 1)
        s_ref[par] = lax.dot_general(qs_ref[...], kb_ref[ca], _NT, preferred_element_type=F32)

    def stage_b(it, par):
        cb = it - 1
        vst = vst_ref[...]
        tie = tie_ref[...]
        for j in range(lane_tiles):
            key = keys_ref[cb, :, j * LANES:(j + 1) * LANES]
            pos = cb * ck + j * LANES + row_lane
            sel = ((key > vst) | ((key == vst) & (pos <= tie))) & (key > INT_MIN)
            bias_ref[:, j * LANES:(j + 1) * LANES] = jnp.where(sel, 0.0, NEG_BIG)
        sb = 1 - par
        for h in range(DSA_HEADS):
            rows = slice(h * DSA_ROWS, (h + 1) * DSA_ROWS)
            s = s_ref[sb, rows, :] + bias_ref[...]
            s_ref[sb, rows, :] = s
            part = s[:, 0:LANES]
            for j in range(1, lane_tiles):
                part = jnp.maximum(part, s[:, j * LANES:(j + 1) * LANES])
            m_old = m_ref[h]
            m_new = jnp.maximum(m_old, jnp.max(part, axis=-1, keepdims=True))
            alpha_ref[sb, h] = jnp.exp(m_old - m_new)
            m_ref[h] = m_new
        for h in range(DSA_HEADS):
            rows = slice(h * DSA_ROWS, (h + 1) * DSA_ROWS)
            p_ref[sb, rows, :] = jnp.exp(s_ref[sb, rows, :] - m_ref[h]).astype(BF16)

    def stage_c(it, par):
        cc = jnp.maximum(it - 2, 0)
        pv = jnp.dot(p_ref[par], vb_ref[cc], preferred_element_type=F32)
        for h in range(DSA_HEADS):
            acc_ref[h] = alpha_ref[par, h] * acc_ref[h] + pv[h * DSA_ROWS:(h + 1) * DSA_ROWS]

    def attend_step(it, carry):
        for par in range(2):
            @pl.when(it % 2 == par)
            def _():
                stage_a(it, par)
                stage_b(it, par)
                stage_c(it, par)
        return carry

    p_ref[1] = jnp.zeros(p_ref.shape[1:], BF16)
    alpha_ref[1] = jnp.ones(alpha_ref.shape[1:], F32)
    stage_a(0, 0)
    lax.fori_loop(1, nck + 1, attend_step, 0)
    for par in range(2):
        pl.when((nck + 1) % 2 == par)(functools.partial(stage_c, nck + 1, par))
    for h in range(DSA_HEADS):
        a = acc_ref[h]
        o_ref[:, h * HEAD_DIM:(h + 1) * HEAD_DIM] = (a[:, :HEAD_DIM] / a[:, HEAD_DIM:HEAD_DIM + 1]).astype(o_ref.dtype)


def _dsa(qb, qi, wi, ki, kb, vb, batch, seq, ck):
    nq = seq // DSA_ROWS
    nc = seq // ck
    k_sel = min(TOPK_MAX, seq // 4)
    seq_bits = int(math.log2(seq))
    assert 2 ** seq_bits == seq and k_sel <= ck and seq % DSA_ROWS == 0
    assert seq // BF16_ROWS <= 256
    assert seq <= 2 ** DIGIT_BITS
    chunked = lambda a: a.reshape(batch, nc, ck, a.shape[-1])
    blk = lambda w: pl.BlockSpec((DSA_ROWS, w), lambda b, n: (b * nq + n, 0))
    whole = lambda w: pl.BlockSpec((None, nc, ck, w), lambda b, n: (b, 0, 0, 0))
    kernel = functools.partial(_dsa_kernel, ck=ck, nc=nc, k_sel=k_sel, seq_bits=seq_bits)
    return pl.pallas_call(
        kernel,
        grid=(batch, nq),
        in_specs=[blk(512), blk(256), blk(128), whole(IDX_DIM), whole(HEAD_DIM), whole(LANES)],
        out_specs=blk(512),
        out_shape=jax.ShapeDtypeStruct((batch * seq, 512), BF16),
        scratch_shapes=[
            pltpu.VMEM((nc, DSA_ROWS, ck), I32),
            pltpu.VMEM((nc, ck, DSA_ROWS), I32),
            pltpu.VMEM((nc, ck * jnp.dtype(BF16).itemsize // 4, DSA_ROWS), jnp.uint32),
            pltpu.VMEM((nc, ck * jnp.dtype(BF16).itemsize // 4, DSA_ROWS), jnp.uint32),
            pltpu.VMEM((DSA_ROWS, LANES), I32),
            pltpu.VMEM((DSA_ROWS, LANES), I32),
            pltpu.VMEM((DSA_ROWS, ck), F32),
            pltpu.VMEM((DSA_HEADS, DSA_ROWS, 1), F32),
            pltpu.VMEM((DSA_HEADS, DSA_ROWS, LANES), F32),
            pltpu.VMEM((DSA_HEADS * DSA_ROWS, HEAD_DIM), BF16),
            pltpu.VMEM((2, DSA_HEADS * DSA_ROWS, ck), F32),
            pltpu.VMEM((2, DSA_HEADS * DSA_ROWS, ck), BF16),
            pltpu.VMEM((2, DSA_HEADS, DSA_ROWS, 1), F32),
        ],
        compiler_params=_cparams(("parallel", "arbitrary")),
        name="dsa",
    )(qb, qi, wi, chunked(ki), chunked(kb), chunked(vb))


def _out_proj_kernel(oa_ref, ob_ref, x_ref, wa_ref, wb_ref, g_ref, b_ref, h_ref, hb_ref):
    mix = jnp.dot(oa_ref[...], wa_ref[...], preferred_element_type=F32)
    mix = mix + jnp.dot(ob_ref[...], wb_ref[...], preferred_element_type=F32)
    h = _layer_norm_rows(ALPHA * x_ref[...] + mix, g_ref[...], b_ref[...])
    h_ref[...] = h
    hb_ref[...] = h.astype(BF16)


def _out_proj(oa, ob, x2, wo_a, wo_b, g, b, tm):
    n = x2.shape[0]
    row = lambda w: pl.BlockSpec((tm, w), lambda i: (i, 0))
    const = lambda r, c: pl.BlockSpec((r, c), lambda i: (0, 0))
    return pl.pallas_call(
        _out_proj_kernel,
        grid=(n // tm,),
        in_specs=[row(512), row(512), row(D_MODEL), const(512, D_MODEL), const(512, D_MODEL),
                  const(1, D_MODEL), const(1, D_MODEL)],
        out_specs=[row(D_MODEL), row(D_MODEL)],
        out_shape=[jax.ShapeDtypeStruct((n, D_MODEL), F32), jax.ShapeDtypeStruct((n, D_MODEL), BF16)],
        compiler_params=_cparams(("parallel",)),
        name="out_proj",
    )(oa, ob, x2, wo_a, wo_b, g, b)


def _top_values(s, count, with_rank):
    vals = []
    rank = jnp.full(s.shape, float(count), F32) if with_rank else None
    cur = s
    for j in range(count):
        m = jnp.max(cur, axis=0, keepdims=True)
        vals.append(m)
        below = cur < m
        cur = jnp.where(below, cur, -jnp.inf)
        if with_rank:
            rank = jnp.where(below, rank, jnp.minimum(rank, float(j)))
    return vals, rank


def _oddeven_merge_sort_pairs(n):
    pairs = []

    def merge(lo, hi, r):
        step = r * 2
        if step < hi - lo:
            merge(lo, hi, step)
            merge(lo + r, hi, step)
            pairs.extend((i, i + r) for i in range(lo + r, hi - r, step))
        else:
            pairs.append((lo, lo + r))

    def sort(lo, hi):
        if hi - lo >= 1:
            mid = lo + (hi - lo) // 2
            sort(lo, mid)
            sort(mid + 1, hi)
            merge(lo, hi, 1)

    sort(0, n - 1)
    return pairs


def _sorted_top16(s):
    n = PEER_NKEYS // SUBLANES
    x = [s[i * SUBLANES:(i + 1) * SUBLANES, :] for i in range(n)]
    for i, j in _oddeven_merge_sort_pairs(n):
        x[i], x[j] = jnp.maximum(x[i], x[j]), jnp.minimum(x[i], x[j])
    for shift in (4, 2, 1):
        y = [jnp.maximum(x[i], pltpu.roll(x[n - 1 - i], shift, 0)) for i in range(n)]
        for stride in (8, 4, 2, 1):
            for i in range(n):
                if i & stride == 0:
                    y[i], y[i + stride] = jnp.maximum(y[i], y[i + stride]), jnp.minimum(y[i], y[i + stride])
        x = y
    return x


def _peer_route_kernel(h_ref, wqt_ref, sk_ref, r2_ref, p2_ref, na_ref, p1_ref):
    qt = lax.dot_general(wqt_ref[...], h_ref[...], _NT, preferred_element_type=F32)
    half = PEER_DKEY // 2
    for h in range(PEER_HEADS):
        s = []
        for p in range(2):
            idx = h * 2 + p
            qhp = qt[idx * half:(idx + 1) * half, :].astype(BF16)
            s.append(jnp.dot(sk_ref[idx], qhp, preferred_element_type=F32))
        s1, s2 = s
        v1 = [x[0:1] for x in _sorted_top16(s1)]
        top2 = _sorted_top16(s2)
        v2 = [x[0:1] for x in top2]
        rank2 = []
        for i in range(PEER_NKEYS // SUBLANES):
            blk = s2[i * SUBLANES:(i + 1) * SUBLANES, :]
            r = jnp.where(blk < top2[0], 1.0, 0.0)
            for j in range(1, PEER_TOPK):
                r = r + jnp.where(blk < top2[j], 1.0, 0.0)
            rank2.append(r)
        rank2 = jnp.concatenate(rank2, axis=0)
        v2_all = jnp.concatenate(v2, axis=0)
        sub = lax.broadcasted_iota(I32, (8, v2_all.shape[1]), 0)
        cands = [v1[0] + v2_all]
        for i in range(1, 8):
            lim = PEER_TOPK // (i + 1)
            cands.append(jnp.where(sub < lim, v1[i] + v2_all[:8], -jnp.inf))
        cands.append(jnp.concatenate(v1[8:], axis=0) + v2[0])
        cand = jnp.concatenate(cands, axis=0)
        tops, _ = _top_values(cand, PEER_TOPK, False)
        tau = tops[-1]
        tau = jnp.where(tau > -jnp.inf, tau, jnp.min(jnp.where(cand > -jnp.inf, cand, jnp.inf), axis=0,
                                                     keepdims=True))
        mx = v1[0] + v2[0]
        z = jnp.sum(jnp.where(cand >= tau, jnp.exp(cand - mx), 0.0), axis=0, keepdims=True)
        na = jnp.zeros(s1.shape, F32)
        for j in range(8):
            na = na + jnp.where((s1 + v2[j]) >= tau, 1.0, 0.0)
        n_top = jnp.sum(jnp.where((v1[0] + v2_all[8:]) >= tau, 1.0, 0.0), axis=0, keepdims=True)
        na = na + jnp.where(s1 == v1[0], n_top, 0.0)
        r2_ref[h] = pltpu.bitcast(rank2.astype(BF16), jnp.uint32)
        p2_ref[h] = pltpu.bitcast(jnp.exp(s2 - v2[0]).astype(BF16), jnp.uint32)
        na_ref[h] = na
        p1_ref[h] = jnp.exp(s1 - v1[0]) / z


def _peer_route(hb, wqt, sk, tb):
    n = hb.shape[0]
    word_rows = PEER_NKEYS * jnp.dtype(BF16).itemsize // 4
    tab16 = jax.ShapeDtypeStruct((PEER_HEADS, word_rows, n), jnp.uint32)
    tab32 = jax.ShapeDtypeStruct((PEER_HEADS, PEER_NKEYS, n), F32)
    spec16 = pl.BlockSpec((PEER_HEADS, word_rows, tb), lambda i: (0, 0, i))
    spec32 = pl.BlockSpec((PEER_HEADS, PEER_NKEYS, tb), lambda i: (0, 0, i))
    return pl.pallas_call(
        _peer_route_kernel,
        grid=(n // tb,),
        in_specs=[pl.BlockSpec((tb, D_MODEL), lambda i: (i, 0)),
                  pl.BlockSpec(wqt.shape, lambda i: (0, 0)),
                  pl.BlockSpec(sk.shape, lambda i: (0, 0, 0))],
        out_specs=[spec16, spec16, spec32, spec32],
        out_shape=[tab16, tab16, tab32, tab32],
        compiler_params=_cparams(("parallel",)),
        name="peer_route",
    )(hb, wqt, sk)


def _peer_ffn_kernel(hb_ref, h_ref, u_ref, vt_ref, r2_ref, p2_ref, na_ref, p1_ref, g_ref, b_ref,
                     o_ref, acc_ref, w_ref, *, tb, eb, sub):
    j = pl.program_id(1)

    @pl.when(j == 0)
    def _():
        acc_ref[...] = jnp.zeros(acc_ref.shape, F32)

    groups = PEER_NKEYS // BF16_ROWS
    wr = r2_ref.shape[1] // groups
    zero = jnp.zeros((), BF16)
    for sb in range(eb // sub):
        srows = slice(sb * sub, (sb + 1) * sub)
        act_t = lax.dot_general(u_ref[srows, :], hb_ref[...], _NT, preferred_element_type=F32)
        for asub in range(sub // PEER_NKEYS):
            al = sb * (sub // PEER_NKEYS) + asub
            for tt in range(tb // LANES):
                cols = slice(tt * LANES, (tt + 1) * LANES)
                gate = [jnp.zeros((BF16_ROWS, LANES), BF16) for _ in range(groups)]
                for h in range(PEER_HEADS):
                    na = jnp.broadcast_to(na_ref[h, al:al + 1, cols], (BF16_ROWS, LANES)).astype(BF16)
                    p1 = jnp.broadcast_to(p1_ref[h, al:al + 1, cols], (BF16_ROWS, LANES)).astype(BF16)
                    for g in range(groups):
                        ws = slice(g * wr, (g + 1) * wr)
                        r2 = pltpu.bitcast(r2_ref[h, ws, cols], BF16)
                        p2 = pltpu.bitcast(p2_ref[h, ws, cols], BF16)
                        gate[g] = gate[g] + jnp.where(r2 < na, p2, zero) * p1
                for g in range(groups):
                    lrows = slice(asub * PEER_NKEYS + g * BF16_ROWS, asub * PEER_NKEYS + (g + 1) * BF16_ROWS)
                    x = act_t[lrows, cols]
                    gelu = 0.5 * x * (1.0 + lax.erf(x * (2.0 ** -0.5)))
                    w_ref[sb, lrows, cols] = gelu.astype(BF16) * gate[g]
    w_all = w_ref[...].reshape(eb, tb)
    acc_ref[...] += jnp.dot(vt_ref[...], w_all, preferred_element_type=F32)

    @pl.when(j == pl.num_programs(1) - 1)
    def _():
        z = ALPHA * h_ref[...] + acc_ref[...].T
        o_ref[...] = _layer_norm_rows(z, g_ref[...], b_ref[...])


def _peer_ffn(hb, h, u, vt, tabs, g, b, tb, eb):
    n = hb.shape[0]
    ne = u.shape[0]
    tab_spec = pl.BlockSpec((PEER_HEADS, tabs[0].shape[1], tb), lambda i, j: (0, 0, i))
    a_spec = pl.BlockSpec((PEER_HEADS, eb // PEER_NKEYS, tb), lambda i, j: (0, j, i))
    sub = 512
    kernel = functools.partial(_peer_ffn_kernel, tb=tb, eb=eb, sub=sub)
    return pl.pallas_call(
        kernel,
        grid=(n // tb, ne // eb),
        in_specs=[pl.BlockSpec((tb, D_MODEL), lambda i, j: (i, 0)),
                  pl.BlockSpec((tb, D_MODEL), lambda i, j: (i, 0)),
                  pl.BlockSpec((eb, D_MODEL), lambda i, j: (j, 0)),
                  pl.BlockSpec((D_MODEL, eb), lambda i, j: (0, j)),
                  tab_spec, tab_spec, a_spec, a_spec,
                  pl.BlockSpec((1, D_MODEL), lambda i, j: (0, 0)),
                  pl.BlockSpec((1, D_MODEL), lambda i, j: (0, 0))],
        out_specs=pl.BlockSpec((tb, D_MODEL), lambda i, j: (i, 0)),
        out_shape=jax.ShapeDtypeStruct((n, D_MODEL), F32),
        scratch_shapes=[pltpu.VMEM((D_MODEL, tb), F32), pltpu.VMEM((eb // sub, sub, tb), BF16)],
        compiler_params=_cparams(("parallel", "arbitrary")),
        name="peer_ffn",
    )(hb, h, u, vt, *tabs, g, b)


def _tiles(n_tokens, seq):
    tm = 256 if seq % 256 == 0 else BLOCK
    ck = 512 if seq >= 2048 else BLOCK
    tb_route = 256 if n_tokens % 256 == 0 else BLOCK
    tb_ffn = 512 if n_tokens % 512 == 0 else BLOCK
    eb = 2048
    return tm, ck, tb_route, tb_ffn, eb


def kernel(x, w_in, idx_k_g, idx_k_b, sinks, w_o, ln1_g, ln1_b, peer_wq, peer_subkeys, peer_u, peer_v,
           ln2_g, ln2_b):
    batch, seq, _ = x.shape
    n = batch * seq
    tm, ck, tb_route, tb_ffn, eb = _tiles(n, seq)

    inv = ROPE_THETA ** (-jnp.arange(0, HEAD_DIM, 2, dtype=F32) / HEAD_DIM)
    ang = jnp.arange(seq, dtype=F32)[:, None] * inv[None, :]
    cos, sin = jnp.cos(ang), jnp.sin(ang)
    cos_t = jnp.concatenate([cos, cos, cos, cos], axis=-1)
    sin_t = jnp.concatenate([-sin, sin, -sin, sin], axis=-1)

    h = x.reshape(n, D_MODEL)
    for l in range(DEPTH):
        w_pad = jnp.pad(w_in[l], ((0, 0), (0, MIX_COLS_PAD - MIX_COLS))).astype(BF16)
        kg = jnp.pad(idx_k_g[l], (0, LANES - IDX_DIM)).reshape(1, LANES)
        kb = jnp.pad(idx_k_b[l], (0, LANES - IDX_DIM)).reshape(1, LANES)
        qa, ka, va, qb, kbb, vb, qi, ki, wi = _in_proj(h, w_pad, cos_t, sin_t, kg, kb, seq, tm)
        oa = _swa(sinks[l], qa, ka, va, batch, seq)
        ob = _dsa(qb, qi, wi, ki, kbb, vb, batch, seq, ck)
        wo = w_o[l].astype(BF16)
        h1, h1b = _out_proj(oa, ob, h, wo[:512], wo[512:], ln1_g[l].reshape(1, -1), ln1_b[l].reshape(1, -1), tm)
        wqt = peer_wq[l].T.astype(BF16)
        sk = peer_subkeys[l].reshape(PEER_HEADS * 2, PEER_NKEYS, PEER_DKEY // 2).astype(BF16)
        tabs = _peer_route(h1b, wqt, sk, tb_route)
        u = peer_u[l].astype(BF16)
        vt = peer_v[l].T.astype(BF16)
        h = _peer_ffn(h1b, h1, u, vt, tabs, ln2_g[l].reshape(1, -1), ln2_b[l].reshape(1, -1), tb_ffn, eb)
    return h.reshape(batch, seq, D_MODEL)
```

```python
import functools
import math

import jax
import jax.numpy as jnp
from jax import lax
from jax.experimental import pallas as pl
from jax.experimental.pallas import tpu as pltpu

F32 = jnp.float32
BF16 = jnp.bfloat16
I32 = jnp.int32

D_MODEL = 1024
HEAD_DIM = 64
SWA_HEADS = 8
SWA_KV_HEADS = 2
WINDOW = 128
BLOCK = 128
DSA_HEADS = 8
IDX_HEADS = 4
IDX_DIM = 64
TOPK_MAX = 256
ROPE_THETA = 10000.0
PEER_HEADS = 8
PEER_NKEYS = 128
PEER_DKEY = 256
PEER_TOPK = 16
LN_EPS = 1e-5
DEPTH = 1
ALPHA = (2.0 * DEPTH) ** 0.25

LANES = 128
SUBLANES = 8
BF16_ROWS = 16
VMEM_LIMIT_BYTES = 56 * 1024 * 1024

MIX_COLS = 1732
MIX_COLS_PAD = 1792
WI_LANE0 = 64

NEG_BIG = -1e30
INT_MIN = -(2 ** 31)

_NT = (((1,), (1,)), ((), ()))


def _cparams(sem):
    return pltpu.CompilerParams(dimension_semantics=sem, vmem_limit_bytes=VMEM_LIMIT_BYTES)


def _layer_norm_rows(z, g, b):
    mu = jnp.mean(z, axis=-1, keepdims=True)
    zc = z - mu
    var = jnp.mean(zc * zc, axis=-1, keepdims=True)
    return zc * lax.rsqrt(var + LN_EPS) * g + b


def _in_proj_kernel(x_ref, w_ref, cos_ref, sin_ref, kg_ref, kb_ref,
                    qa_ref, ka_ref, va_ref, qb_ref, kb_out_ref, vb_ref, qi_ref, ki_ref, wi_ref):
    xb = x_ref[...].astype(BF16)
    proj = jnp.dot(xb, w_ref[...], preferred_element_type=F32)
    cos = cos_ref[...]
    sin = sin_ref[...]
    lane = lax.broadcasted_iota(I32, cos.shape, 1)
    first_half = (lane % HEAD_DIM) < (HEAD_DIM // 2)

    def rope(t):
        swapped = jnp.where(first_half, pltpu.roll(t, LANES - HEAD_DIM // 2, 1),
                            pltpu.roll(t, HEAD_DIM // 2, 1))
        return t * cos + swapped * sin

    def slab(i):
        return proj[:, i * LANES:(i + 1) * LANES]

    q_scale = HEAD_DIM ** -0.5
    for i in range(4):
        qa_ref[:, i * LANES:(i + 1) * LANES] = (rope(slab(i)) * q_scale).astype(BF16)
        qb_ref[:, i * LANES:(i + 1) * LANES] = (rope(slab(6 + i)) * q_scale).astype(BF16)
    ka_ref[...] = rope(slab(4)).astype(BF16)
    va_ref[...] = slab(5).astype(BF16)
    kv = slab(10)
    kv_r = rope(kv)
    kb_out_ref[...] = kv_r[:, :HEAD_DIM].astype(BF16)
    vb_ref[...] = jnp.where(lane < HEAD_DIM, pltpu.roll(kv, HEAD_DIM, 1),
                            jnp.where(lane == HEAD_DIM, 1.0, 0.0)).astype(BF16)
    for i in range(2):
        qi_ref[:, i * LANES:(i + 1) * LANES] = rope(slab(11 + i)).astype(BF16)
    last = slab(13)
    is_ki = lane < IDX_DIM
    mu = jnp.sum(jnp.where(is_ki, last, 0.0), axis=-1, keepdims=True) * (1.0 / IDX_DIM)
    cen = last - mu
    var = jnp.sum(jnp.where(is_ki, cen * cen, 0.0), axis=-1, keepdims=True) * (1.0 / IDX_DIM)
    kin = cen * lax.rsqrt(var + LN_EPS) * kg_ref[...] + kb_ref[...]
    kin = jnp.where(is_ki, kin, 0.0)
    ki_ref[...] = rope(kin)[:, :IDX_DIM].astype(BF16)
    w_scale = (IDX_HEADS ** -0.5) * (IDX_DIM ** -0.5)
    wi_ref[...] = last * w_scale


def _in_proj(x2, w_pad, cos_t, sin_t, kg, kb, seq, tm):
    n = x2.shape[0]
    nt = n // tm
    tps = seq // tm
    row = lambda w: pl.BlockSpec((tm, w), lambda i: (i, 0))
    out_shapes = [
        jax.ShapeDtypeStruct((n, 512), BF16), jax.ShapeDtypeStruct((n, 128), BF16),
        jax.ShapeDtypeStruct((n, 128), BF16), jax.ShapeDtypeStruct((n, 512), BF16),
        jax.ShapeDtypeStruct((n, 64), BF16), jax.ShapeDtypeStruct((n, 128), BF16),
        jax.ShapeDtypeStruct((n, 256), BF16), jax.ShapeDtypeStruct((n, 64), BF16),
        jax.ShapeDtypeStruct((n, 128), F32),
    ]
    return pl.pallas_call(
        _in_proj_kernel,
        grid=(nt,),
        in_specs=[
            row(D_MODEL),
            pl.BlockSpec((D_MODEL, MIX_COLS_PAD), lambda i: (0, 0)),
            pl.BlockSpec((tm, LANES), lambda i: (i % tps, 0)),
            pl.BlockSpec((tm, LANES), lambda i: (i % tps, 0)),
            pl.BlockSpec((1, LANES), lambda i: (0, 0)),
            pl.BlockSpec((1, LANES), lambda i: (0, 0)),
        ],
        out_specs=[row(512), row(128), row(128), row(512), row(64), row(128), row(256), row(64), row(128)],
        out_shape=out_shapes,
        compiler_params=_cparams(("parallel",)),
        name="in_proj",
    )(x2, w_pad, cos_t, sin_t, kg, kb)


def _swa_kernel(sink_ref, q_ref, kp_ref, kc_ref, vp_ref, vc_ref, o_ref):
    n = pl.program_id(1)
    qi = lax.broadcasted_iota(I32, (BLOCK, 2 * BLOCK), 0)
    sj = lax.broadcasted_iota(I32, (BLOCK, 2 * BLOCK), 1)
    diff = qi + BLOCK - sj
    valid = (diff >= 0) & (diff < WINDOW) & ((n > 0) | (sj >= BLOCK))
    group = SWA_HEADS // SWA_KV_HEADS
    for kh in range(SWA_KV_HEADS):
        cs = slice(kh * HEAD_DIM, (kh + 1) * HEAD_DIM)
        k = jnp.concatenate([kp_ref[:, cs], kc_ref[:, cs]], axis=0)
        v = jnp.concatenate([vp_ref[:, cs], vc_ref[:, cs]], axis=0)
        for g in range(group):
            h = kh * group + g
            hs = slice(h * HEAD_DIM, (h + 1) * HEAD_DIM)
            s = lax.dot_general(q_ref[:, hs], k, _NT, preferred_element_type=F32)
            s = jnp.where(valid, s, NEG_BIG)
            sink = sink_ref[h]
            m = jnp.maximum(jnp.max(s, axis=-1, keepdims=True), sink)
            p = jnp.exp(s - m)
            denom = jnp.sum(p, axis=-1, keepdims=True) + jnp.exp(sink - m)
            o = jnp.dot(p.astype(BF16), v, preferred_element_type=F32)
            o_ref[:, hs] = (o / denom).astype(o_ref.dtype)


def _swa(sinks, qa, ka, va, batch, seq):
    nb = seq // BLOCK
    cur = lambda w: pl.BlockSpec((BLOCK, w), lambda b, n: (b * nb + n, 0))
    prev = lambda w: pl.BlockSpec((BLOCK, w), lambda b, n: (b * nb + jnp.maximum(n - 1, 0), 0))
    return pl.pallas_call(
        _swa_kernel,
        grid=(batch, nb),
        in_specs=[pl.BlockSpec(memory_space=pltpu.SMEM), cur(512), prev(128), cur(128), prev(128), cur(128)],
        out_specs=cur(512),
        out_shape=jax.ShapeDtypeStruct((batch * seq, 512), BF16),
        compiler_params=_cparams(("parallel", "parallel")),
        name="swa",
    )(sinks, qa, ka, ka, va, va)


DSA_ROWS = 2 * BLOCK
DIGIT_BITS = 14


def _dsa_kernel(qb_ref, qi_ref, wi_ref, ki_ref, kb_ref, vb_ref, o_ref,
                keys_ref, keyt_ref, dig1_ref, dig2_ref, vst_ref, tie_ref, bias_ref, m_ref, acc_ref, qs_ref,
                s_ref, p_ref, alpha_ref,
                *, ck, nc, k_sel, seq_bits):
    n = pl.program_id(1)
    t0 = n * DSA_ROWS
    nck = (t0 + DSA_ROWS + ck - 1) // ck
    lane_tiles = ck // LANES
    qpos = t0 + lax.broadcasted_iota(I32, (DSA_ROWS, ck), 0)
    lane_pos = lax.broadcasted_iota(I32, (DSA_ROWS, ck), 1)

    qi = qi_ref[...]
    qi4 = jnp.concatenate([qi[:, h * IDX_DIM:(h + 1) * IDX_DIM] for h in range(IDX_HEADS)], axis=0)
    wi = wi_ref[...]
    wcols = [wi[:, WI_LANE0 + h:WI_LANE0 + h + 1] for h in range(IDX_HEADS)]

    def score_chunk(c, causal):
        r = lax.dot_general(qi4, ki_ref[c], _NT, preferred_element_type=F32)
        r = jnp.maximum(r, 0.0)
        isc = wcols[0] * r[0:DSA_ROWS]
        for h in range(1, IDX_HEADS):
            isc = isc + wcols[h] * r[h * DSA_ROWS:(h + 1) * DSA_ROWS]
        isc = jnp.where(isc == 0.0, 0.0, isc)
        bits = pltpu.bitcast(isc, I32)
        key = bits ^ ((bits >> 31) & 0x7FFFFFFF)
        if causal:
            key = jnp.where(c * ck + lane_pos <= qpos, key, INT_MIN)
        keys_ref[c] = key
        key_t = pltpu.bitcast(pltpu.bitcast(key, F32).T, I32)
        keyt_ref[c] = key_t
        key_u = key_t ^ INT_MIN
        dig1_ref[c] = pltpu.bitcast(digit_pattern(lax.shift_right_logical(key_u, 32 - DIGIT_BITS)), jnp.uint32)
        dig2_ref[c] = pltpu.bitcast(digit_pattern(lax.shift_right_logical(key_u, 32 - 2 * DIGIT_BITS)
                                                  & (2 ** DIGIT_BITS - 1)), jnp.uint32)

    def digit_pattern(d):
        return pltpu.bitcast(lax.shift_left(d + 128, 16), F32).astype(BF16)

    def scan_chunks(lo, hi, causal):
        def body(c, carry):
            score_chunk(c, causal)
            return carry
        lax.fori_loop(lo, hi, body, 0)

    n_past = (t0 + 1) // ck
    scan_chunks(0, n_past, False)
    scan_chunks(n_past, nck, True)

    sub_pos = lax.broadcasted_iota(I32, (SUBLANES, LANES), 0)
    groups = DSA_ROWS // LANES

    def to_rows(x):
        wide = jnp.broadcast_to(pltpu.bitcast(x[0:1, :], F32), (LANES, LANES))
        return pltpu.bitcast(wide.T, I32)

    def select(v):
        def count(g, pred):
            parts = [jnp.zeros((SUBLANES, LANES), I32) for _ in range(4)]
            i = 0
            for c in range(v):
                for kb in range(ck // SUBLANES):
                    tile = keyt_ref[c, kb * SUBLANES:(kb + 1) * SUBLANES, g * LANES:(g + 1) * LANES]
                    hit = pred(tile, c * ck + kb * SUBLANES + sub_pos)
                    parts[i % 4] = parts[i % 4] + jnp.where(hit, 1, 0)
                    i += 1
            tot = jnp.sum((parts[0] + parts[1]) + (parts[2] + parts[3]), axis=0, keepdims=True)
            return jnp.broadcast_to(tot, (SUBLANES, LANES))

        def bit_step(i, ts):
            bit = lax.shift_left(jnp.int32(1), 31 - i)
            out = []
            for g in range(groups):
                cand = ts[g] | bit
                cand_s = cand ^ INT_MIN
                out.append(jnp.where(count(g, lambda key, pos: key >= cand_s) >= k_sel, cand, ts[g]))
            return tuple(out)

        prow = SUBLANES * 4 // jnp.dtype(BF16).itemsize
        wrow = SUBLANES
        one16 = jnp.ones((), BF16)
        zero16 = jnp.zeros((), BF16)

        def pattern_rows(code):
            return digit_pattern(jnp.broadcast_to(code[0:1, :], (prow, LANES)))

        def count_packed(ref, g, pred):
            parts = [jnp.zeros((prow, LANES), BF16) for _ in range(4)]
            i = 0
            for c in range(v):
                for kb in range(ref.shape[1] // wrow):
                    tile = pltpu.bitcast(ref[c, kb * wrow:(kb + 1) * wrow, g * LANES:(g + 1) * LANES], BF16)
                    parts[i % 4] = parts[i % 4] + jnp.where(pred(tile), one16, zero16)
                    i += 1
            tot = jnp.sum(((parts[0] + parts[1]) + (parts[2] + parts[3])).astype(F32), axis=0, keepdims=True)
            return jnp.broadcast_to(tot, (SUBLANES, LANES))

        def digit_search(ref, base):
            def step(i, ds):
                bit = lax.shift_left(jnp.int32(1), DIGIT_BITS - 1 - i)
                out = []
                for g in range(groups):
                    cand = ds[g] | bit
                    pat = pattern_rows(cand)
                    cnt = base[g] + count_packed(ref, g, lambda t: t >= pat)
                    out.append(jnp.where(cnt >= k_sel, cand, ds[g]))
                return tuple(out)
            return lax.fori_loop(0, DIGIT_BITS, step, (zi,) * groups)

        zi = jnp.zeros((SUBLANES, LANES), I32)
        zf = jnp.zeros((SUBLANES, LANES), F32)
        d1 = digit_search(dig1_ref, (zf,) * groups)
        above, d2_base = [], []
        for g in range(groups):
            pat1 = pattern_rows(d1[g])
            d2_base.append(count_packed(dig1_ref, g, lambda t: t > pat1))
            for c in range(v):
                for kb in range(dig1_ref.shape[1] // wrow):
                    rows = slice(kb * wrow, (kb + 1) * wrow)
                    cols = slice(g * LANES, (g + 1) * LANES)
                    same = pltpu.bitcast(dig1_ref[c, rows, cols], BF16) == pat1
                    kept = jnp.where(same, pltpu.bitcast(dig2_ref[c, rows, cols], BF16), zero16)
                    dig2_ref[c, rows, cols] = pltpu.bitcast(kept, jnp.uint32)
        d2 = digit_search(dig2_ref, tuple(d2_base))
        ts = tuple(lax.shift_left(d1[g], 32 - DIGIT_BITS) | lax.shift_left(d2[g], 32 - 2 * DIGIT_BITS)
                   for g in range(groups))
        ts = lax.fori_loop(2 * DIGIT_BITS, 32, bit_step, ts)
        vstar = [t ^ INT_MIN for t in ts]
        need, surplus = [], []
        for g in range(groups):
            vs = vstar[g]
            vst_ref[g * LANES:(g + 1) * LANES, :] = to_rows(vs)
            tie_ref[g * LANES:(g + 1) * LANES, :] = jnp.full((LANES, LANES), 2 ** seq_bits, I32)
            cnt_gt = count(g, lambda key, pos: key > vs)
            cnt_eq = count(g, lambda key, pos: key == vs)
            need.append(k_sel - cnt_gt)
            surplus.append(jnp.max(cnt_eq - need[g]))

        pos_bits = (v * ck - 1).bit_length()
        for g in range(groups):
            @pl.when(surplus[g] > 0)
            def _(g=g):
                cols = slice(g * LANES, (g + 1) * LANES)
                vs = jnp.broadcast_to(vstar[g][0:1, :], (prow, LANES))
                pack_pos = lax.broadcasted_iota(I32, (prow, LANES), 0)
                for c in range(v):
                    for kb in range(ck // prow):
                        key = keyt_ref[c, kb * prow:(kb + 1) * prow, cols]
                        tied_pos = jnp.where(key == vs, c * ck + kb * prow + pack_pos, 2 ** seq_bits)
                        dig2_ref[c, kb * wrow:(kb + 1) * wrow, cols] = pltpu.bitcast(digit_pattern(tied_pos),
                                                                                     jnp.uint32)
                need_f = need[g].astype(F32)

                def idx_step(i, p):
                    cand = p | lax.shift_left(jnp.int32(1), pos_bits - 1 - i)
                    pat = pattern_rows(cand)
                    cnt = count_packed(dig2_ref, g, lambda t: t < pat)
                    return jnp.where(cnt < need_f, cand, p)

                tie_ref[cols, :] = to_rows(lax.fori_loop(0, pos_bits, idx_step, zi))

    for v in range(1, nc + 1):
        pl.when(nck == v)(functools.partial(select, v))

    qb = qb_ref[...]
    for h in range(DSA_HEADS):
        qs_ref[h * DSA_ROWS:(h + 1) * DSA_ROWS, :] = qb[:, h * HEAD_DIM:(h + 1) * HEAD_DIM]
    m_ref[...] = jnp.full(m_ref.shape, NEG_BIG, F32)
    acc_ref[...] = jnp.zeros(acc_ref.shape, F32)
    row_lane = lax.broadcasted_iota(I32, (DSA_ROWS, LANES), 1)

    def stage_a(it, par):
        ca = jnp.minimum(it, nck - 1)
        s_ref[par] = lax.dot_general(qs_ref[...], kb_ref[ca], _NT, preferred_element_type=F32)

    def stage_b(it, par):
        cb = it - 1
        vst = vst_ref[...]
        tie = tie_ref[...]
        for j in range(lane_tiles):
            key = keys_ref[cb, :, j * LANES:(j + 1) * LANES]
            pos = cb * ck + j * LANES + row_lane
            sel = ((key > vst) | ((key == vst) & (pos <= tie))) & (key > INT_MIN)
            bias_ref[:, j * LANES:(j + 1) * LANES] = jnp.where(sel, 0.0, NEG_BIG)
        sb = 1 - par
        for h in range(DSA_HEADS):
            rows = slice(h * DSA_ROWS, (h + 1) * DSA_ROWS)
            s = s_ref[sb, rows, :] + bias_ref[...]
            s_ref[sb, rows, :] = s
            part = s[:, 0:LANES]
            for j in range(1, lane_tiles):
                part = jnp.maximum(part, s[:, j * LANES:(j + 1) * LANES])
            m_old = m_ref[h]
            m_new = jnp.maximum(m_old, jnp.max(part, axis=-1, keepdims=True))
            alpha_ref[sb, h] = jnp.exp(m_old - m_new)
            m_ref[h] = m_new
        for h in range(DSA_HEADS):
            rows = slice(h * DSA_ROWS, (h + 1) * DSA_ROWS)
            p_ref[sb, rows, :] = jnp.exp(s_ref[sb, rows, :] - m_ref[h]).astype(BF16)

    def stage_c(it, par):
        cc = jnp.maximum(it - 2, 0)
        pv = jnp.dot(p_ref[par], vb_ref[cc], preferred_element_type=F32)
        for h in range(DSA_HEADS):
            acc_ref[h] = alpha_ref[par, h] * acc_ref[h] + pv[h * DSA_ROWS:(h + 1) * DSA_ROWS]

    def attend_step(it, carry):
        for par in range(2):
            @pl.when(it % 2 == par)
            def _():
                stage_a(it, par)
                stage_b(it, par)
                stage_c(it, par)
        return carry

    p_ref[1] = jnp.zeros(p_ref.shape[1:], BF16)
    alpha_ref[1] = jnp.ones(alpha_ref.shape[1:], F32)
    stage_a(0, 0)
    lax.fori_loop(1, nck + 1, attend_step, 0)
    for par in range(2):
        pl.when((nck + 1) % 2 == par)(functools.partial(stage_c, nck + 1, par))
    for h in range(DSA_HEADS):
        a = acc_ref[h]
        o_ref[:, h * HEAD_DIM:(h + 1) * HEAD_DIM] = (a[:, :HEAD_DIM] / a[:, HEAD_DIM:HEAD_DIM + 1]).astype(o_ref.dtype)


def _dsa(qb, qi, wi, ki, kb, vb, batch, seq, ck):
    nq = seq // DSA_ROWS
    nc = seq // ck
    k_sel = min(TOPK_MAX, seq // 4)
    seq_bits = int(math.log2(seq))
    assert 2 ** seq_bits == seq and k_sel <= ck and seq % DSA_ROWS == 0
    assert seq // BF16_ROWS <= 256
    assert seq <= 2 ** DIGIT_BITS
    chunked = lambda a: a.reshape(batch, nc, ck, a.shape[-1])
    blk = lambda w: pl.BlockSpec((DSA_ROWS, w), lambda b, n: (b * nq + n, 0))
    whole = lambda w: pl.BlockSpec((None, nc, ck, w), lambda b, n: (b, 0, 0, 0))
    kernel = functools.partial(_dsa_kernel, ck=ck, nc=nc, k_sel=k_sel, seq_bits=seq_bits)
    return pl.pallas_call(
        kernel,
        grid=(batch, nq),
        in_specs=[blk(512), blk(256), blk(128), whole(IDX_DIM), whole(HEAD_DIM), whole(LANES)],
        out_specs=blk(512),
        out_shape=jax.ShapeDtypeStruct((batch * seq, 512), BF16),
        scratch_shapes=[
            pltpu.VMEM((nc, DSA_ROWS, ck), I32),
            pltpu.VMEM((nc, ck, DSA_ROWS), I32),
            pltpu.VMEM((nc, ck * jnp.dtype(BF16).itemsize // 4, DSA_ROWS), jnp.uint32),
            pltpu.VMEM((nc, ck * jnp.dtype(BF16).itemsize // 4, DSA_ROWS), jnp.uint32),
            pltpu.VMEM((DSA_ROWS, LANES), I32),
            pltpu.VMEM((DSA_ROWS, LANES), I32),
            pltpu.VMEM((DSA_ROWS, ck), F32),
            pltpu.VMEM((DSA_HEADS, DSA_ROWS, 1), F32),
            pltpu.VMEM((DSA_HEADS, DSA_ROWS, LANES), F32),
            pltpu.VMEM((DSA_HEADS * DSA_ROWS, HEAD_DIM), BF16),
            pltpu.VMEM((2, DSA_HEADS * DSA_ROWS, ck), F32),
            pltpu.VMEM((2, DSA_HEADS * DSA_ROWS, ck), BF16),
            pltpu.VMEM((2, DSA_HEADS, DSA_ROWS, 1), F32),
        ],
        compiler_params=_cparams(("parallel", "arbitrary")),
        name="dsa",
    )(qb, qi, wi, chunked(ki), chunked(kb), chunked(vb))


def _out_proj_kernel(oa_ref, ob_ref, x_ref, wa_ref, wb_ref, g_ref, b_ref, h_ref, hb_ref):
    mix = jnp.dot(oa_ref[...], wa_ref[...], preferred_element_type=F32)
    mix = mix + jnp.dot(ob_ref[...], wb_ref[...], preferred_element_type=F32)
    h = _layer_norm_rows(ALPHA * x_ref[...] + mix, g_ref[...], b_ref[...])
    h_ref[...] = h
    hb_ref[...] = h.T.astype(BF16)


def _out_proj(oa, ob, x2, wo_a, wo_b, g, b, tm):
    n = x2.shape[0]
    row = lambda w: pl.BlockSpec((tm, w), lambda i: (i, 0))
    const = lambda r, c: pl.BlockSpec((r, c), lambda i: (0, 0))
    return pl.pallas_call(
        _out_proj_kernel,
        grid=(n // tm,),
        in_specs=[row(512), row(512), row(D_MODEL), const(512, D_MODEL), const(512, D_MODEL),
                  const(1, D_MODEL), const(1, D_MODEL)],
        out_specs=[row(D_MODEL), pl.BlockSpec((D_MODEL, tm), lambda i: (0, i))],
        out_shape=[jax.ShapeDtypeStruct((n, D_MODEL), F32), jax.ShapeDtypeStruct((D_MODEL, n), BF16)],
        compiler_params=_cparams(("parallel",)),
        name="out_proj",
    )(oa, ob, x2, wo_a, wo_b, g, b)


def _top_values(s, count, with_rank):
    vals = []
    rank = jnp.full(s.shape, float(count), F32) if with_rank else None
    cur = s
    for j in range(count):
        m = jnp.max(cur, axis=0, keepdims=True)
        vals.append(m)
        below = cur < m
        cur = jnp.where(below, cur, -jnp.inf)
        if with_rank:
            rank = jnp.where(below, rank, jnp.minimum(rank, float(j)))
    return vals, rank


def _oddeven_merge_sort_pairs(n):
    pairs = []

    def merge(lo, hi, r):
        step = r * 2
        if step < hi - lo:
            merge(lo, hi, step)
            merge(lo + r, hi, step)
            pairs.extend((i, i + r) for i in range(lo + r, hi - r, step))
        else:
            pairs.append((lo, lo + r))

    def sort(lo, hi):
        if hi - lo >= 1:
            mid = lo + (hi - lo) // 2
            sort(lo, mid)
            sort(mid + 1, hi)
            merge(lo, hi, 1)

    sort(0, n - 1)
    return pairs


def _sorted_top16(s):
    n = PEER_NKEYS // SUBLANES
    x = [s[i * SUBLANES:(i + 1) * SUBLANES, :] for i in range(n)]
    for i, j in _oddeven_merge_sort_pairs(n):
        x[i], x[j] = jnp.maximum(x[i], x[j]), jnp.minimum(x[i], x[j])
    for shift in (4, 2, 1):
        y = [jnp.maximum(x[i], pltpu.roll(x[n - 1 - i], shift, 0)) for i in range(n)]
        for stride in (8, 4, 2, 1):
            for i in range(n):
                if i & stride == 0:
                    y[i], y[i + stride] = jnp.maximum(y[i], y[i + stride]), jnp.minimum(y[i], y[i + stride])
        x = y
    return x


def _peer_route_kernel(h_ref, wqt_ref, sk_ref, r2_ref, p2_ref, na_ref, p1_ref):
    qt = jnp.dot(wqt_ref[...], h_ref[...], preferred_element_type=F32)
    half = PEER_DKEY // 2
    for h in range(PEER_HEADS):
        s = []
        for p in range(2):
            idx = h * 2 + p
            qhp = qt[idx * half:(idx + 1) * half, :].astype(BF16)
            s.append(jnp.dot(sk_ref[idx], qhp, preferred_element_type=F32))
        s1, s2 = s
        v1 = [x[0:1] for x in _sorted_top16(s1)]
        top2 = _sorted_top16(s2)
        v2 = [x[0:1] for x in top2]
        rank2 = []
        for i in range(PEER_NKEYS // SUBLANES):
            blk = s2[i * SUBLANES:(i + 1) * SUBLANES, :]
            r = jnp.where(blk < top2[0], 1.0, 0.0)
            for j in range(1, PEER_TOPK):
                r = r + jnp.where(blk < top2[j], 1.0, 0.0)
            rank2.append(r)
        rank2 = jnp.concatenate(rank2, axis=0)
        v2_all = jnp.concatenate(v2, axis=0)
        sub = lax.broadcasted_iota(I32, (8, v2_all.shape[1]), 0)
        cands = [v1[0] + v2_all]
        for i in range(1, 8):
            lim = PEER_TOPK // (i + 1)
            cands.append(jnp.where(sub < lim, v1[i] + v2_all[:8], -jnp.inf))
        cands.append(jnp.concatenate(v1[8:], axis=0) + v2[0])
        cand = jnp.concatenate(cands, axis=0)
        tops, _ = _top_values(cand, PEER_TOPK, False)
        tau = tops[-1]
        tau = jnp.where(tau > -jnp.inf, tau, jnp.min(jnp.where(cand > -jnp.inf, cand, jnp.inf), axis=0,
                                                     keepdims=True))
        mx = v1[0] + v2[0]
        z = jnp.sum(jnp.where(cand >= tau, jnp.exp(cand - mx), 0.0), axis=0, keepdims=True)
        na = jnp.zeros(s1.shape, F32)
        for j in range(8):
            na = na + jnp.where((s1 + v2[j]) >= tau, 1.0, 0.0)
        n_top = jnp.sum(jnp.where((v1[0] + v2_all[8:]) >= tau, 1.0, 0.0), axis=0, keepdims=True)
        na = na + jnp.where(s1 == v1[0], n_top, 0.0)
        r2_ref[h] = pltpu.bitcast(rank2.astype(BF16), jnp.uint32)
        p2_ref[h] = pltpu.bitcast(jnp.exp(s2 - v2[0]).astype(BF16), jnp.uint32)
        na_ref[h] = na
        p1_ref[h] = jnp.exp(s1 - v1[0]) / z


def _peer_route(hb, wqt, sk, tb):
    n = hb.shape[1]
    word_rows = PEER_NKEYS * jnp.dtype(BF16).itemsize // 4
    tab16 = jax.ShapeDtypeStruct((PEER_HEADS, word_rows, n), jnp.uint32)
    tab32 = jax.ShapeDtypeStruct((PEER_HEADS, PEER_NKEYS, n), F32)
    spec16 = pl.BlockSpec((PEER_HEADS, word_rows, tb), lambda i: (0, 0, i))
    spec32 = pl.BlockSpec((PEER_HEADS, PEER_NKEYS, tb), lambda i: (0, 0, i))
    return pl.pallas_call(
        _peer_route_kernel,
        grid=(n // tb,),
        in_specs=[pl.BlockSpec((D_MODEL, tb), lambda i: (0, i)),
                  pl.BlockSpec(wqt.shape, lambda i: (0, 0)),
                  pl.BlockSpec(sk.shape, lambda i: (0, 0, 0))],
        out_specs=[spec16, spec16, spec32, spec32],
        out_shape=[tab16, tab16, tab32, tab32],
        compiler_params=_cparams(("parallel",)),
        name="peer_route",
    )(hb, wqt, sk)


def _peer_ffn_kernel(hb_ref, h_ref, u_ref, vt_ref, r2_ref, p2_ref, na_ref, p1_ref, g_ref, b_ref,
                     o_ref, acc_ref, w_ref, *, tb, eb, sub):
    j = pl.program_id(1)

    @pl.when(j == 0)
    def _():
        acc_ref[...] = jnp.zeros(acc_ref.shape, F32)

    groups = PEER_NKEYS // BF16_ROWS
    wr = r2_ref.shape[1] // groups
    zero = jnp.zeros((), BF16)
    for sb in range(eb // sub):
        srows = slice(sb * sub, (sb + 1) * sub)
        act_t = jnp.dot(u_ref[srows, :], hb_ref[...], preferred_element_type=F32)
        for asub in range(sub // PEER_NKEYS):
            al = sb * (sub // PEER_NKEYS) + asub
            for tt in range(tb // LANES):
                cols = slice(tt * LANES, (tt + 1) * LANES)
                gate = [jnp.zeros((BF16_ROWS, LANES), BF16) for _ in range(groups)]
                for h in range(PEER_HEADS):
                    na = jnp.broadcast_to(na_ref[h, al:al + 1, cols], (BF16_ROWS, LANES)).astype(BF16)
                    p1 = jnp.broadcast_to(p1_ref[h, al:al + 1, cols], (BF16_ROWS, LANES)).astype(BF16)
                    for g in range(groups):
                        ws = slice(g * wr, (g + 1) * wr)
                        r2 = pltpu.bitcast(r2_ref[h, ws, cols], BF16)
                        p2 = pltpu.bitcast(p2_ref[h, ws, cols], BF16)
                        gate[g] = gate[g] + jnp.where(r2 < na, p2, zero) * p1
                for g in range(groups):
                    lrows = slice(asub * PEER_NKEYS + g * BF16_ROWS, asub * PEER_NKEYS + (g + 1) * BF16_ROWS)
                    x = act_t[lrows, cols]
                    gelu = 0.5 * x * (1.0 + lax.erf(x * (2.0 ** -0.5)))
                    w_ref[sb, lrows, cols] = gelu.astype(BF16) * gate[g]
    w_all = w_ref[...].reshape(eb, tb)
    acc_ref[...] += jnp.dot(vt_ref[...], w_all, preferred_element_type=F32)

    @pl.when(j == pl.num_programs(1) - 1)
    def _():
        z = ALPHA * h_ref[...] + acc_ref[...].T
        o_ref[...] = _layer_norm_rows(z, g_ref[...], b_ref[...])


def _peer_ffn(hb, h, u, vt, tabs, g, b, tb, eb):
    n = hb.shape[1]
    ne = u.shape[0]
    tab_spec = pl.BlockSpec((PEER_HEADS, tabs[0].shape[1], tb), lambda i, j: (0, 0, i))
    a_spec = pl.BlockSpec((PEER_HEADS, eb // PEER_NKEYS, tb), lambda i, j: (0, j, i))
    sub = 512
    kernel = functools.partial(_peer_ffn_kernel, tb=tb, eb=eb, sub=sub)
    return pl.pallas_call(
        kernel,
        grid=(n // tb, ne // eb),
        in_specs=[pl.BlockSpec((D_MODEL, tb), lambda i, j: (0, i)),
                  pl.BlockSpec((tb, D_MODEL), lambda i, j: (i, 0)),
                  pl.BlockSpec((eb, D_MODEL), lambda i, j: (j, 0)),
                  pl.BlockSpec((D_MODEL, eb), lambda i, j: (0, j)),
                  tab_spec, tab_spec, a_spec, a_spec,
                  pl.BlockSpec((1, D_MODEL), lambda i, j: (0, 0)),
                  pl.BlockSpec((1, D_MODEL), lambda i, j: (0, 0))],
        out_specs=pl.BlockSpec((tb, D_MODEL), lambda i, j: (i, 0)),
        out_shape=jax.ShapeDtypeStruct((n, D_MODEL), F32),
        scratch_shapes=[pltpu.VMEM((D_MODEL, tb), F32), pltpu.VMEM((eb // sub, sub, tb), BF16)],
        compiler_params=_cparams(("parallel", "arbitrary")),
        name="peer_ffn",
    )(hb, h, u, vt, *tabs, g, b)


def _tiles(n_tokens, seq):
    tm = 256 if seq % 256 == 0 else BLOCK
    ck = 512 if seq >= 2048 else BLOCK
    tb_route = 256 if n_tokens % 256 == 0 else BLOCK
    tb_ffn = 512 if n_tokens % 512 == 0 else BLOCK
    eb = 2048
    return tm, ck, tb_route, tb_ffn, eb


def kernel(x, w_in, idx_k_g, idx_k_b, sinks, w_o, ln1_g, ln1_b, peer_wq, peer_subkeys, peer_u, peer_v,
           ln2_g, ln2_b):
    batch, seq, _ = x.shape
    n = batch * seq
    tm, ck, tb_route, tb_ffn, eb = _tiles(n, seq)

    inv = ROPE_THETA ** (-jnp.arange(0, HEAD_DIM, 2, dtype=F32) / HEAD_DIM)
    ang = jnp.arange(seq, dtype=F32)[:, None] * inv[None, :]
    cos, sin = jnp.cos(ang), jnp.sin(ang)
    cos_t = jnp.concatenate([cos, cos, cos, cos], axis=-1)
    sin_t = jnp.concatenate([-sin, sin, -sin, sin], axis=-1)

    h = x.reshape(n, D_MODEL)
    for l in range(DEPTH):
        w_pad = jnp.pad(w_in[l], ((0, 0), (0, MIX_COLS_PAD - MIX_COLS))).astype(BF16)
        kg = jnp.pad(idx_k_g[l], (0, LANES - IDX_DIM)).reshape(1, LANES)
        kb = jnp.pad(idx_k_b[l], (0, LANES - IDX_DIM)).reshape(1, LANES)
        qa, ka, va, qb, kbb, vb, qi, ki, wi = _in_proj(h, w_pad, cos_t, sin_t, kg, kb, seq, tm)
        oa = _swa(sinks[l], qa, ka, va, batch, seq)
        ob = _dsa(qb, qi, wi, ki, kbb, vb, batch, seq, ck)
        wo = w_o[l].astype(BF16)
        h1, h1b = _out_proj(oa, ob, h, wo[:512], wo[512:], ln1_g[l].reshape(1, -1), ln1_b[l].reshape(1, -1), tm)
        wqt = peer_wq[l].T.astype(BF16)
        sk = peer_subkeys[l].reshape(PEER_HEADS * 2, PEER_NKEYS, PEER_DKEY // 2).astype(BF16)
        tabs = _peer_route(h1b, wqt, sk, tb_route)
        u = peer_u[l].astype(BF16)
        vt = peer_v[l].T.astype(BF16)
        h = _peer_ffn(h1b, h1, u, vt, tabs, ln2_g[l].reshape(1, -1), ln2_b[l].reshape(1, -1), tb_ffn, eb)
    return h.reshape(batch, seq, D_MODEL)
```
